```python
import jax, jax.numpy as jnp
from jax import lax
import numpy as np

D_MODEL = 2048
BATCH = 4
SEQ = 4096
DEPTH = 2

CTX_LEN = 256
GRID_W = 64
N_MIXERS = 2
N_ATTN = (DEPTH + 1) // 2
N_REC = DEPTH // 2
NORM_EPS = 1e-6
N_MOD = 6

HEAD_DIM = 128
N_Q_HEADS = D_MODEL // HEAD_DIM
N_KV_HEADS = N_Q_HEADS // 4
GQA_GROUP = N_Q_HEADS // N_KV_HEADS
WINDOW = 128
ATTN_BLOCK = 128
ROPE_BASE = 10000.0
Q_DIM = N_Q_HEADS * HEAD_DIM
KV_DIM = N_KV_HEADS * HEAD_DIM
QKV_DIM = Q_DIM + 2 * KV_DIM

D_RNN = D_MODEL
RNN_BLOCKS = 8
RNN_BLOCK_W = D_RNN // RNN_BLOCKS
CONV_W = 4
CONV_LEFT = 1
LRU_C = 8.0

PEER_HEADS = 8
PEER_KEY_DIM = 128
N_KEYS = 128
N_EXPERTS = N_KEYS * N_KEYS
PEER_TOPK = 16
PEER_CHUNK = 128

kernel_name = "hybrid_swa_rglru_peer_diffusion_block"


def rmsnorm(x, gain):
    x32 = x.astype(jnp.float32)
    y = x32 * lax.rsqrt(jnp.mean(x32 * x32, axis=-1, keepdims=True) + NORM_EPS)
    return y.astype(x.dtype) * gain


def modulate(h, shift, scale):
    return h * (1.0 + scale) + shift


def axial_rope(rows, dtype):
    t = jnp.arange(rows * GRID_W)
    row = (t // GRID_W).astype(jnp.float32)
    col = (t % GRID_W).astype(jnp.float32)
    half = HEAD_DIM // 2
    inv = ROPE_BASE ** (-jnp.arange(0, half, 2, dtype=jnp.float32) / half)
    ang_r = row[:, None] * inv[None, :]
    ang_c = col[:, None] * inv[None, :]
    ang = jnp.concatenate([ang_r, ang_r, ang_c, ang_c], axis=-1)
    return jnp.cos(ang).astype(dtype), jnp.sin(ang).astype(dtype)


def apply_rope(x, cos, sin):
    half = HEAD_DIM // 2
    qtr = half // 2

    def rot(z):
        return jnp.concatenate([-z[..., qtr:], z[..., :qtr]], axis=-1)

    rotated = jnp.concatenate([rot(x[..., :half]), rot(x[..., half:])], axis=-1)
    return x * cos[None, :, None, :] + rotated * sin[None, :, None, :]


def attn_mixer(hx, hc, cos, sin, w_qkv, w_o, sink, need_ctx_out):
    B, S, _ = hx.shape
    C = hc.shape[1]
    nb = S // ATTN_BLOCK
    dt = hx.dtype
    scale = HEAD_DIM ** -0.5

    def project(h):
        qkv = h @ w_qkv
        q, k, v = jnp.split(qkv, [Q_DIM, Q_DIM + KV_DIM], axis=-1)
        T = h.shape[1]
        return (q.reshape(B, T, N_Q_HEADS, HEAD_DIM), k.reshape(B, T, N_KV_HEADS, HEAD_DIM),
                v.reshape(B, T, N_KV_HEADS, HEAD_DIM))

    qx, kx, vx = project(hx)
    qc, kc, vc = project(hc)
    qx = apply_rope(qx, cos, sin).reshape(B, nb, ATTN_BLOCK, N_KV_HEADS, GQA_GROUP, HEAD_DIM)
    kx = apply_rope(kx, cos, sin)
    sink32 = sink.astype(jnp.float32).reshape(N_KV_HEADS, GQA_GROUP)

    def band(t):
        tp = jnp.pad(t, ((0, 0), (ATTN_BLOCK, ATTN_BLOCK), (0, 0), (0, 0)))
        tp = tp.reshape(B, nb + 2, ATTN_BLOCK, N_KV_HEADS, HEAD_DIM)
        return jnp.concatenate([tp[:, :-2], tp[:, 1:-1], tp[:, 2:]], axis=2)

    kb, vb = band(kx), band(vx)
    s_w = jnp.einsum('bnqhgd,bnkhd->bnhgqk', qx, kb).astype(jnp.float32) * scale
    blk = jnp.arange(nb)[:, None]
    qpos = blk * ATTN_BLOCK + jnp.arange(ATTN_BLOCK)[None, :]
    kpos = (blk - 1) * ATTN_BLOCK + jnp.arange(3 * ATTN_BLOCK)[None, :]
    valid = ((kpos[:, None, :] >= 0) & (kpos[:, None, :] < S)
             & (jnp.abs(qpos[:, :, None] - kpos[:, None, :]) <= WINDOW))
    s_w = jnp.where(valid[None, :, None, None], s_w, -jnp.inf)
    s_c = jnp.einsum('bnqhgd,bkhd->bnhgqk', qx, kc).astype(jnp.float32) * scale
    sink_b = sink32[None, None, :, :, None]
    m = jnp.maximum(jnp.maximum(s_w.max(-1), s_c.max(-1)), sink_b)
    p_w = jnp.exp(s_w - m[..., None])
    p_c = jnp.exp(s_c - m[..., None])
    denom = p_w.sum(-1) + p_c.sum(-1) + jnp.exp(sink_b - m)
    o = (jnp.einsum('bnhgqk,bnkhd->bnqhgd', p_w.astype(dt), vb)
         + jnp.einsum('bnhgqk,bkhd->bnqhgd', p_c.astype(dt), vc))
    o = (o / jnp.moveaxis(denom, -1, 2)[..., None]).astype(dt)
    out_x = o.reshape(B, S, Q_DIM) @ w_o

    out_c = None
    if need_ctx_out:
        qc = qc.reshape(B, C, N_KV_HEADS, GQA_GROUP, HEAD_DIM)
        s_cc = jnp.einsum('bqhgd,bkhd->bhgqk', qc, kc).astype(jnp.float32) * scale
        sink_col = jnp.broadcast_to(sink32[None, :, :, None, None], (B, N_KV_HEADS, GQA_GROUP, C, 1))
        p = jax.nn.softmax(jnp.concatenate([s_cc, sink_col], axis=-1), axis=-1)[..., :C]
        o_c = jnp.einsum('bhgqk,bkhd->bqhgd', p.astype(dt), vc)
        out_c = o_c.reshape(B, C, Q_DIM) @ w_o
    return out_x, out_c


def short_conv(u, w, b):
    T = u.shape[1]
    up = jnp.pad(u, ((0, 0), (CONV_LEFT, CONV_W - 1 - CONV_LEFT), (0, 0)))
    y = b
    for k in range(CONV_W):
        y = y + up[:, k:k + T] * w[k]
    return y


def block_diag(u, w, b):
    ub = u.reshape(u.shape[:-1] + (RNN_BLOCKS, RNN_BLOCK_W))
    return jnp.einsum('...ni,nio->...no', ub, w).reshape(u.shape) + b


def rglru_coeffs(u, w_a, b_a, w_x, b_x, lam):
    r = jax.nn.sigmoid(block_diag(u, w_a, b_a).astype(jnp.float32))
    i = jax.nn.sigmoid(block_diag(u, w_x, b_x).astype(jnp.float32))
    log_a = -LRU_C * r * jax.nn.softplus(-lam.astype(jnp.float32))
    a = jnp.exp(log_a)
    bx = jnp.sqrt(-jnp.expm1(2.0 * log_a)) * i * u.astype(jnp.float32)
    return a, bx


def bidir_scan(a_f, b_f, a_b, b_b, h0):
    A = jnp.stack([a_f, jnp.flip(a_b, 1)], axis=0)
    X = jnp.stack([b_f, jnp.flip(b_b, 1)], axis=0)

    def step(h, ab):
        a, bx = ab
        h = a * h + bx
        return h, h

    h_last, ys = lax.scan(step, h0, (jnp.moveaxis(A, 2, 0), jnp.moveaxis(X, 2, 0)))
    ys = jnp.moveaxis(ys, 0, 2)
    return ys[0] + jnp.flip(ys[1], 1), h_last


def rec_mixer(hx, hc, w_in, conv_w, conv_b, w_a, b_a, w_x, b_x, lam, w_out, need_ctx_out):
    B = hx.shape[0]
    dt = hx.dtype
    w_gate, w_u = w_in[:, :D_RNN], w_in[:, D_RNN:]

    def coeffs(h):
        u = short_conv(h @ w_u, conv_w, conv_b)
        fa, fb = rglru_coeffs(u, w_a[0], b_a[0], w_x[0], b_x[0], lam[0])
        ba, bb = rglru_coeffs(u, w_a[1], b_a[1], w_x[1], b_x[1], lam[1])
        return fa, fb, ba, bb

    h0 = jnp.zeros((2, B, D_RNN), jnp.float32)
    yc, hc_final = bidir_scan(*coeffs(hc), h0)
    yx, _ = bidir_scan(*coeffs(hx), hc_final)
    out_x = (yx.astype(dt) * jax.nn.gelu(hx @ w_gate)) @ w_out
    out_c = None
    if need_ctx_out:
        out_c = (yc.astype(dt) * jax.nn.gelu(hc @ w_gate)) @ w_out
    return out_x, out_c


def peer(h, w_q, keys, u_tab, v_tab):
    T, D = h.shape
    q = (h @ w_q).reshape(T, PEER_HEADS, 2, PEER_KEY_DIM)
    s = jnp.einsum('thpd,pnd->thpn', q, keys).astype(jnp.float32)
    s1, i1 = lax.top_k(s[:, :, 0], PEER_TOPK)
    s2, i2 = lax.top_k(s[:, :, 1], PEER_TOPK)
    cand = (s1[..., :, None] + s2[..., None, :]).reshape(T, PEER_HEADS, PEER_TOPK * PEER_TOPK)
    cidx = (i1[..., :, None] * N_KEYS + i2[..., None, :]).reshape(T, PEER_HEADS, PEER_TOPK * PEER_TOPK)
    top_s, sel = lax.top_k(cand, PEER_TOPK)
    idx = jnp.take_along_axis(cidx, sel, axis=-1).reshape(T, PEER_HEADS * PEER_TOPK)
    g = jax.nn.softmax(top_s, axis=-1).astype(h.dtype).reshape(T, PEER_HEADS * PEER_TOPK)
    n = T // PEER_CHUNK

    def chunk(args):
        hc, ic, gc = args
        z = jnp.einsum('tkd,td->tk', u_tab[ic], hc)
        act = jax.nn.gelu(z) * gc
        return jnp.einsum('tk,tkd->td', act, v_tab[ic])

    out = lax.map(chunk, (h.reshape(n, PEER_CHUNK, D), idx.reshape(n, PEER_CHUNK, -1),
                          g.reshape(n, PEER_CHUNK, -1)))
    return out.reshape(T, D)


def setup_inputs(seed: int = 0) -> dict:
    key = jax.random.key(seed)
    ks = jax.random.split(key, 32)
    f32 = jnp.float32

    def nrm(k, shape, s):
        return jax.random.normal(k, shape, f32) * s

    a0 = jax.random.uniform(ks[20], (N_REC, 2, D_RNN), f32, minval=0.9, maxval=0.999)
    return {
        "x": nrm(ks[0], (BATCH, SEQ, D_MODEL), 1.0),
        "c": nrm(ks[1], (BATCH, D_MODEL), 1.0),
        "ctx": nrm(ks[2], (BATCH, CTX_LEN, D_MODEL), 1.0),
        "c_ctx": nrm(ks[3], (D_MODEL,), 1.0),
        "w_mod": nrm(ks[4], (DEPTH, D_MODEL, N_MOD * D_MODEL), 0.3 * D_MODEL ** -0.5),
        "b_mod": nrm(ks[5], (DEPTH, N_MOD * D_MODEL), 0.02),
        "norm_mix": 1.0 + nrm(ks[6], (DEPTH, D_MODEL), 0.02),
        "norm_ffn": 1.0 + nrm(ks[7], (DEPTH, D_MODEL), 0.02),
        "norm_final": 1.0 + nrm(ks[8], (D_MODEL,), 0.02),
        "attn_w_qkv": nrm(ks[9], (N_ATTN, D_MODEL, QKV_DIM), D_MODEL ** -0.5),
        "attn_w_o": nrm(ks[10], (N_ATTN, Q_DIM, D_MODEL), Q_DIM ** -0.5),
        "attn_sink": nrm(ks[11], (N_ATTN, N_Q_HEADS), 0.5),
        "rec_w_in": nrm(ks[12], (N_REC, D_MODEL, 2 * D_RNN), D_MODEL ** -0.5),
        "rec_conv_w": nrm(ks[13], (N_REC, CONV_W, D_RNN), CONV_W ** -0.5),
        "rec_conv_b": nrm(ks[14], (N_REC, D_RNN), 0.02),
        "rec_w_a": nrm(ks[15], (N_REC, 2, RNN_BLOCKS, RNN_BLOCK_W, RNN_BLOCK_W), RNN_BLOCK_W ** -0.5),
        "rec_b_a": nrm(ks[16], (N_REC, 2, D_RNN), 0.02),
        "rec_w_x": nrm(ks[17], (N_REC, 2, RNN_BLOCKS, RNN_BLOCK_W, RNN_BLOCK_W), RNN_BLOCK_W ** -0.5),
        "rec_b_x": nrm(ks[18], (N_REC, 2, D_RNN), 0.02),
        "rec_lambda": jnp.log(a0) - jnp.log1p(-a0),
        "rec_w_out": nrm(ks[19], (N_REC, D_RNN, D_MODEL), D_RNN ** -0.5),
        "peer_w_q": nrm(ks[21], (DEPTH, D_MODEL, PEER_HEADS * 2 * PEER_KEY_DIM), D_MODEL ** -0.5),
        "peer_keys": nrm(ks[22], (DEPTH, 2, N_KEYS, PEER_KEY_DIM), PEER_KEY_DIM ** -0.5),
        "peer_u": nrm(ks[23], (DEPTH, N_EXPERTS, D_MODEL), D_MODEL ** -0.5),
        "peer_v": nrm(ks[24], (DEPTH, N_EXPERTS, D_MODEL), 0.5),
    }


def reference(x, c, ctx, c_ctx, w_mod, b_mod, norm_mix, norm_ffn, norm_final,
              attn_w_qkv, attn_w_o, attn_sink,
              rec_w_in, rec_conv_w, rec_conv_b, rec_w_a, rec_b_a, rec_w_x, rec_b_x, rec_lambda, rec_w_out,
              peer_w_q, peer_keys, peer_u, peer_v):
    B, S, D = x.shape
    C = ctx.shape[1]
    rows = S // GRID_W
    cos, sin = axial_rope(rows, x.dtype)
    xs, cs = x, ctx
    for i in range(DEPTH):
        last = i == DEPTH - 1
        j = i // N_MIXERS
        mod_x = (jax.nn.silu(c) @ w_mod[i] + b_mod[i]).reshape(B, N_MOD, 1, D)
        mod_c = (jax.nn.silu(c_ctx) @ w_mod[i] + b_mod[i]).reshape(N_MOD, 1, D)

        hx = modulate(rmsnorm(xs, norm_mix[i]), mod_x[:, 0], mod_x[:, 1])
        hc = modulate(rmsnorm(cs, norm_mix[i]), mod_c[0], mod_c[1])
        if i % N_MIXERS == 0:
            yx, yc = attn_mixer(hx, hc, cos, sin, attn_w_qkv[j], attn_w_o[j], attn_sink[j], not last)
        else:
            yx, yc = rec_mixer(hx, hc, rec_w_in[j], rec_conv_w[j], rec_conv_b[j], rec_w_a[j], rec_b_a[j],
                               rec_w_x[j], rec_b_x[j], rec_lambda[j], rec_w_out[j], not last)
        xs = xs + mod_x[:, 2] * yx
        if not last:
            cs = cs + mod_c[2] * yc

        hx = modulate(rmsnorm(xs, norm_ffn[i]), mod_x[:, 3], mod_x[:, 4])
        if last:
            fx = peer(hx.reshape(B * S, D), peer_w_q[i], peer_keys[i], peer_u[i], peer_v[i]).reshape(B, S, D)
            xs = xs + mod_x[:, 5] * fx
        else:
            hc = modulate(rmsnorm(cs, norm_ffn[i]), mod_c[3], mod_c[4])
            tokens = jnp.concatenate([hx.reshape(B * S, D), hc.reshape(B * C, D)], axis=0)
            f = peer(tokens, peer_w_q[i], peer_keys[i], peer_u[i], peer_v[i])
            xs = xs + mod_x[:, 5] * f[:B * S].reshape(B, S, D)
            cs = cs + mod_c[5] * f[B * S:].reshape(B, C, D)
    return rmsnorm(xs, norm_final)
```

```python
import functools
import math

import jax
import jax.numpy as jnp
from jax import lax
from jax.experimental import pallas as pl
from jax.experimental.pallas import tpu as pltpu

F32 = jnp.float32
BF16 = jnp.bfloat16

NORM_EPS = 1e-6
N_MOD = 6
GRID_W = 64
HEAD_DIM = 128
N_Q_HEADS = 16
N_KV_HEADS = 4
GQA_GROUP = N_Q_HEADS // N_KV_HEADS
ATTN_BLOCK = 128
ROPE_BASE = 10000.0
RNN_BLOCKS = 8
CONV_W = 4
CONV_LEFT = 1
LRU_C = 8.0
PEER_HEADS = 8
PEER_KEY_DIM = 128
N_KEYS = 128
PEER_TOPK = 16

V7X_VMEM_BYTES = 64 * 1024 * 1024
SUBLANES = 8
LANES = 128
VMEM_CAP = V7X_VMEM_BYTES - 6 * 1024 * 1024


def _params(sem, vmem_bytes):
    limit = int(min(VMEM_CAP, max(32 * 1024 * 1024, vmem_bytes * 5 // 4)))
    return pltpu.CompilerParams(dimension_semantics=sem, vmem_limit_bytes=limit)


def _tile(n, pref):
    return pref if n % pref == 0 else n


def _norm_mod(x, gain, shift, scale):
    ms = jnp.mean(x * x, axis=-1, keepdims=True)
    y = x * lax.rsqrt(ms + NORM_EPS)
    return (y * gain) * (1.0 + scale) + shift


def _dot_nt(a, b):
    return lax.dot_general(a, b, (((1,), (1,)), ((), ())), preferred_element_type=F32)


def _mod_body(c_ref, w_ref, b_ref, o_ref):
    s = jax.nn.silu(c_ref[...])
    o_ref[0] = jnp.dot(s.astype(BF16), w_ref[0].astype(BF16), preferred_element_type=F32) + b_ref[0]


def _mod_vectors(cvec, w_mod, b_mod):
    depth, d, n = w_mod.shape
    r = cvec.shape[0]
    tn = _tile(n, 1536)
    vm = 2 * d * tn * 4 + d * tn * 2 + 4 * r * tn * 4
    return pl.pallas_call(
        _mod_body,
        out_shape=jax.ShapeDtypeStruct((depth, r, n), F32),
        grid=(depth, n // tn),
        in_specs=[
            pl.BlockSpec((r, d), lambda i, j: (0, 0)),
            pl.BlockSpec((1, d, tn), lambda i, j: (i, 0, j)),
            pl.BlockSpec((1, 1, tn), lambda i, j: (i, 0, j)),
        ],
        out_specs=pl.BlockSpec((1, r, tn), lambda i, j: (i, 0, j)),
        compiler_params=_params(("arbitrary", "arbitrary"), vm),
        name="mod_vectors",
    )(cvec, w_mod, b_mod.reshape(depth, 1, n))


def _rope(x, cos, sin_signed):
    lane = lax.broadcasted_iota(jnp.int32, x.shape, 1)
    qtr = HEAD_DIM // 4
    first = (lane % (2 * qtr)) < qtr
    rot = jnp.where(first, pltpu.roll(x, HEAD_DIM - qtr, 1), pltpu.roll(x, qtr, 1))
    return x * cos + rot * sin_signed


def _proj_body(*refs, rope_blocks):
    if rope_blocks:
        x_ref, g_ref, sh_ref, sc_ref, w_ref, cos_ref, sin_ref, o_ref, h_scr = refs
    else:
        x_ref, g_ref, sh_ref, sc_ref, w_ref, o_ref, h_scr = refs
    j = pl.program_id(2)

    @pl.when(j == 0)
    def _():
        h_scr[...] = _norm_mod(x_ref[...], g_ref[...], sh_ref[...], sc_ref[...]).astype(BF16)

    y = jnp.dot(h_scr[...], w_ref[...], preferred_element_type=F32)
    if rope_blocks:

        @pl.when(j < rope_blocks)
        def _():
            cos = cos_ref[...]
            sin = sin_ref[...]
            parts = [
                _rope(y[:, k * HEAD_DIM:(k + 1) * HEAD_DIM], cos, sin)
                for k in range(y.shape[1] // HEAD_DIM)
            ]
            o_ref[...] = jnp.concatenate(parts, axis=1).astype(o_ref.dtype)

        @pl.when(j >= rope_blocks)
        def _():
            o_ref[...] = y.astype(o_ref.dtype)
    else:
        o_ref[...] = y.astype(o_ref.dtype)


def _proj(x, gain, shift, scale, w, out_dtype, tn, rope=None):
    b, t, d = x.shape
    n = w.shape[1]
    tm = _tile(t, 512)
    in_specs = [
        pl.BlockSpec((None, tm, d), lambda bb, i, j: (bb, i, 0)),
        pl.BlockSpec((1, d), lambda bb, i, j: (0, 0)),
        pl.BlockSpec((None, 1, d), lambda bb, i, j: (bb, 0, 0)),
        pl.BlockSpec((None, 1, d), lambda bb, i, j: (bb, 0, 0)),
        pl.BlockSpec((d, tn), lambda bb, i, j: (0, j)),
    ]
    args = [x, gain.reshape(1, d), shift, scale, w]
    rope_blocks = 0
    if rope is not None:
        cos, sin_signed, n_cols = rope
        assert n_cols % tn == 0
        rope_blocks = n_cols // tn
        in_specs += [pl.BlockSpec((tm, HEAD_DIM), lambda bb, i, j: (i, 0))] * 2
        args += [cos, sin_signed]
    osz = jnp.dtype(out_dtype).itemsize
    vm = 2 * tm * d * 4 + 2 * d * tn * 2 + 2 * tm * tn * osz + tm * d * 2 + 3 * tm * tn * 4
    return pl.pallas_call(
        functools.partial(_proj_body, rope_blocks=rope_blocks),
        out_shape=jax.ShapeDtypeStruct((b, t, n), out_dtype),
        grid=(b, t // tm, n // tn),
        in_specs=in_specs,
        out_specs=pl.BlockSpec((None, tm, tn), lambda bb, i, j: (bb, i, j)),
        scratch_shapes=[pltpu.VMEM((tm, d), BF16)],
        compiler_params=_params(("arbitrary", "arbitrary", "arbitrary"), vm),
        name="norm_mod_proj",
    )(*args)


def _oproj_body(a_ref, w_ref, res_ref, gate_ref, o_ref):
    y = jnp.dot(a_ref[...], w_ref[...], preferred_element_type=F32)
    o_ref[...] = res_ref[...] + gate_ref[...] * y


def _rec_oproj_body(yf_ref, yb_ref, gu_ref, w_ref, res_ref, gate_ref, o_ref):
    a = (yf_ref[...] + yb_ref[...]) * jax.nn.gelu(gu_ref[...])
    y = jnp.dot(a.astype(BF16), w_ref[...], preferred_element_type=F32)
    o_ref[...] = res_ref[...] + gate_ref[...] * y


def _oproj(a, w, res, gate):
    b, t, k = a.shape
    d = w.shape[1]
    tm = _tile(t, 512)
    vm = 2 * tm * k * 2 + 2 * k * d * 2 + 4 * tm * d * 4 + tm * d * 4
    return pl.pallas_call(
        _oproj_body,
        out_shape=jax.ShapeDtypeStruct((b, t, d), F32),
        grid=(b, t // tm),
        in_specs=[
            pl.BlockSpec((None, tm, k), lambda bb, i: (bb, i, 0)),
            pl.BlockSpec((k, d), lambda bb, i: (0, 0)),
            pl.BlockSpec((None, tm, d), lambda bb, i: (bb, i, 0)),
            pl.BlockSpec((None, 1, d), lambda bb, i: (bb, 0, 0)),
        ],
        out_specs=pl.BlockSpec((None, tm, d), lambda bb, i: (bb, i, 0)),
        compiler_params=_params(("arbitrary", "arbitrary"), vm),
        name="oproj_residual",
    )(a, w, res, gate)


def _rec_oproj(yf, yb, gu, w, res, gate):
    b, t, k = yf.shape
    d = w.shape[1]
    tm = _tile(t, 256)
    vm = 6 * tm * k * 4 + 2 * k * d * 2 + 4 * tm * d * 4 + 3 * tm * d * 4
    return pl.pallas_call(
        _rec_oproj_body,
        out_shape=jax.ShapeDtypeStruct((b, t, d), F32),
        grid=(b, t // tm),
        in_specs=[
            pl.BlockSpec((None, tm, k), lambda bb, i: (bb, i, 0)),
            pl.BlockSpec((None, tm, k), lambda bb, i: (bb, i, 0)),
            pl.BlockSpec((None, tm, k), lambda bb, i: (bb, i, 0)),
            pl.BlockSpec((k, d), lambda bb, i: (0, 0)),
            pl.BlockSpec((None, tm, d), lambda bb, i: (bb, i, 0)),
            pl.BlockSpec((None, 1, d), lambda bb, i: (bb, 0, 0)),
        ],
        out_specs=pl.BlockSpec((None, tm, d), lambda bb, i: (bb, i, 0)),
        compiler_params=_params(("arbitrary", "arbitrary"), vm),
        name="rec_oproj_residual",
    )(yf, yb, gu, w, res, gate)


def _stack_groups(q):
    return jnp.concatenate(
        [q[:, g * HEAD_DIM:(g + 1) * HEAD_DIM] for g in range(GQA_GROUP)], axis=0)


def _unstack_groups(o, rows):
    return jnp.concatenate([o[g * rows:(g + 1) * rows] for g in range(GQA_GROUP)], axis=1)


def _sink_column(sink_ref, h, rows):
    return jnp.concatenate(
        [jnp.full((rows, 1), sink_ref[h * GQA_GROUP + g], F32) for g in range(GQA_GROUP)], axis=0)


def _attn_body(sink_ref, q_ref, kp_ref, kc_ref, kn_ref, vp_ref, vc_ref, vn_ref, kx_ref, vx_ref,
               o_ref, *, nb):
    i = pl.program_id(1)
    h = pl.program_id(2)
    blk = ATTN_BLOCK
    scale = HEAD_DIM ** -0.5
    qs = _stack_groups(q_ref[...])
    r = lax.broadcasted_iota(jnp.int32, (GQA_GROUP * blk, blk), 0) % blk
    c = lax.broadcasted_iota(jnp.int32, (GQA_GROUP * blk, blk), 1)
    neg = -jnp.inf
    s_p = jnp.where((c >= r) & (i >= 1), _dot_nt(qs, kp_ref[...]) * scale, neg)
    s_c = _dot_nt(qs, kc_ref[...]) * scale
    s_n = jnp.where((c <= r) & (i + 1 < nb), _dot_nt(qs, kn_ref[...]) * scale, neg)
    s_x = _dot_nt(qs, kx_ref[...]) * scale
    sink = _sink_column(sink_ref, h, blk)
    m = jnp.maximum(
        jnp.maximum(jnp.maximum(s_p.max(-1, keepdims=True), s_c.max(-1, keepdims=True)),
                    jnp.maximum(s_n.max(-1, keepdims=True), s_x.max(-1, keepdims=True))),
        sink)
    p_p = jnp.exp(s_p - m)
    p_c = jnp.exp(s_c - m)
    p_n = jnp.exp(s_n - m)
    p_x = jnp.exp(s_x - m)
    denom = (p_p.sum(-1, keepdims=True) + p_c.sum(-1, keepdims=True)
             + p_n.sum(-1, keepdims=True) + p_x.sum(-1, keepdims=True) + jnp.exp(sink - m))
    o = (jnp.dot(p_p.astype(BF16), vp_ref[...], preferred_element_type=F32)
         + jnp.dot(p_c.astype(BF16), vc_ref[...], preferred_element_type=F32)
         + jnp.dot(p_n.astype(BF16), vn_ref[...], preferred_element_type=F32)
         + jnp.dot(p_x.astype(BF16), vx_ref[...], preferred_element_type=F32))
    o_ref[...] = _unstack_groups(o / denom, blk).astype(o_ref.dtype)


def _attention(qkv_x, qkv_c, sink):
    b, s, _ = qkv_x.shape
    c = qkv_c.shape[1]
    nb = s // ATTN_BLOCK
    qw = GQA_GROUP * HEAD_DIM
    k0 = N_Q_HEADS
    v0 = N_Q_HEADS + N_KV_HEADS
    blk = ATTN_BLOCK

    def kv_spec(col0, off):
        def idx(bb, i, h):
            return (bb, jnp.clip(i + off, 0, nb - 1), col0 + h)
        return pl.BlockSpec((None, blk, HEAD_DIM), idx)

    in_specs = [
        pl.BlockSpec(memory_space=pltpu.SMEM),
        pl.BlockSpec((None, blk, qw), lambda bb, i, h: (bb, i, h)),
        kv_spec(k0, -1), kv_spec(k0, 0), kv_spec(k0, 1),
        kv_spec(v0, -1), kv_spec(v0, 0), kv_spec(v0, 1),
        pl.BlockSpec((None, c, HEAD_DIM), lambda bb, i, h: (bb, 0, k0 + h)),
        pl.BlockSpec((None, c, HEAD_DIM), lambda bb, i, h: (bb, 0, v0 + h)),
    ]
    return pl.pallas_call(
        functools.partial(_attn_body, nb=nb),
        out_shape=jax.ShapeDtypeStruct((b, s, N_Q_HEADS * HEAD_DIM), BF16),
        grid=(b, nb, N_KV_HEADS),
        in_specs=in_specs,
        out_specs=pl.BlockSpec((None, blk, qw), lambda bb, i, h: (bb, i, h)),
        compiler_params=_params(("arbitrary", "arbitrary", "arbitrary"), 16 * 1024 * 1024),
        name="window_attention",
    )(sink, qkv_x, qkv_x, qkv_x, qkv_x, qkv_x, qkv_x, qkv_x, qkv_c, qkv_c)


def _ctx_attn_body(sink_ref, q_ref, k_ref, v_ref, o_ref):
    h = pl.program_id(1)
    rows = q_ref.shape[0]
    scale = HEAD_DIM ** -0.5
    qs = _stack_groups(q_ref[...])
    s = _dot_nt(qs, k_ref[...]) * scale
    sink = _sink_column(sink_ref, h, rows)
    m = jnp.maximum(s.max(-1, keepdims=True), sink)
    p = jnp.exp(s - m)
    denom = p.sum(-1, keepdims=True) + jnp.exp(sink - m)
    o = jnp.dot((p / denom).astype(BF16), v_ref[...], preferred_element_type=F32)
    o_ref[...] = _unstack_groups(o, rows).astype(o_ref.dtype)


def _ctx_attention(qkv_c, sink):
    b, c, _ = qkv_c.shape
    qw = GQA_GROUP * HEAD_DIM
    k0 = N_Q_HEADS
    v0 = N_Q_HEADS + N_KV_HEADS
    return pl.pallas_call(
        _ctx_attn_body,
        out_shape=jax.ShapeDtypeStruct((b, c, N_Q_HEADS * HEAD_DIM), BF16),
        grid=(b, N_KV_HEADS),
        in_specs=[
            pl.BlockSpec(memory_space=pltpu.SMEM),
            pl.BlockSpec((None, c, qw), lambda bb, h: (bb, 0, h)),
            pl.BlockSpec((None, c, HEAD_DIM), lambda bb, h: (bb, 0, k0 + h)),
            pl.BlockSpec((None, c, HEAD_DIM), lambda bb, h: (bb, 0, v0 + h)),
        ],
        out_specs=pl.BlockSpec((None, c, qw), lambda bb, h: (bb, 0, h)),
        compiler_params=_params(("arbitrary", "arbitrary"), 16 * 1024 * 1024),
        name="context_attention",
    )(sink, qkv_c, qkv_c, qkv_c)


def _block_diag(ub, w_ref, bias):
    bw = w_ref.shape[1]
    return jnp.concatenate(
        [jnp.dot(ub[:, n * bw:(n + 1) * bw], w_ref[n], preferred_element_type=F32)
         for n in range(w_ref.shape[0])], axis=1) + bias


def _lru_body(up_ref, uc_ref, un_ref, cw_ref, cb_ref, wa_ref, ba_ref, wx_ref, bx_ref, lam_ref,
              h0_ref, y_ref, hl_ref, h_scr, a_scr, b_scr, *, nt, reverse):
    i = pl.program_id(1)
    ti = (nt - 1 - i) if reverse else i
    tm = uc_ref.shape[0]
    halo = up_ref.shape[0]

    @pl.when(i == 0)
    def _():
        h_scr[...] = h0_ref[...]

    prev = jnp.where(ti > 0, up_ref[...], 0.0)
    nxt = jnp.where(ti < nt - 1, un_ref[...], 0.0)
    ext = jnp.concatenate([prev, uc_ref[...], nxt], axis=0)
    u = cb_ref[...]
    for k in range(CONV_W):
        off = halo - CONV_LEFT + k
        u = u + ext[off:off + tm] * cw_ref[k:k + 1, :]

    ub = u.astype(BF16)
    r = jax.nn.sigmoid(_block_diag(ub, wa_ref, ba_ref[...]))
    ig = jax.nn.sigmoid(_block_diag(ub, wx_ref, bx_ref[...]))
    nl = -lam_ref[...]
    softplus = jnp.maximum(nl, 0.0) + jnp.log1p(jnp.exp(-jnp.abs(nl)))
    log_a = -LRU_C * r * softplus
    a_scr[...] = jnp.exp(log_a)
    b_scr[...] = jnp.sqrt(1.0 - jnp.exp(2.0 * log_a)) * ig * u

    ng = tm // SUBLANES
    row = lax.broadcasted_iota(jnp.int32, (SUBLANES, a_scr.shape[1]), 0)

    def group(g, h):
        gi = (ng - 1 - g) if reverse else g
        r0 = pl.multiple_of(gi * SUBLANES, SUBLANES)
        a = a_scr[pl.ds(r0, SUBLANES), :]
        bb = b_scr[pl.ds(r0, SUBLANES), :]
        for k in (1, 2, 4):
            if reverse:
                keep = row < SUBLANES - k
                shift = SUBLANES - k
            else:
                keep = row >= k
                shift = k
            a_sh = pltpu.roll(a, shift, 0)
            b_sh = pltpu.roll(bb, shift, 0)
            bb = bb + a * jnp.where(keep, b_sh, 0.0)
            a = a * jnp.where(keep, a_sh, 1.0)
        y = bb + a * h
        y_ref[pl.ds(r0, SUBLANES), :] = y
        return y[0:1, :] if reverse else y[SUBLANES - 1:SUBLANES, :]

    h_last = lax.fori_loop(0, ng, group, h_scr[...])
    h_scr[...] = h_last

    @pl.when(i == nt - 1)
    def _():
        hl_ref[...] = h_last


def _lru_scan(gu, conv_w, conv_b, w_a, b_a, w_x, b_x, lam, h0, reverse):
    b, t, d2 = gu.shape
    d = d2 // 2
    tm = _tile(t, 256)
    nt = t // tm
    halo = SUBLANES
    hb = tm // halo
    nh = t // halo

    def tmap(i):
        return (nt - 1 - i) if reverse else i

    vec = lambda: pl.BlockSpec((1, d), lambda bb, i: (0, 0))
    in_specs = [
        pl.BlockSpec((None, halo, d), lambda bb, i: (bb, jnp.maximum(tmap(i) * hb - 1, 0), 1)),
        pl.BlockSpec((None, tm, d), lambda bb, i: (bb, tmap(i), 1)),
        pl.BlockSpec((None, halo, d), lambda bb, i: (bb, jnp.minimum((tmap(i) + 1) * hb, nh - 1), 1)),
        pl.BlockSpec((CONV_W, d), lambda bb, i: (0, 0)),
        vec(),
        pl.BlockSpec(w_a.shape, lambda bb, i: (0, 0, 0)),
        vec(),
        pl.BlockSpec(w_x.shape, lambda bb, i: (0, 0, 0)),
        vec(),
        vec(),
        pl.BlockSpec((None, 1, d), lambda bb, i: (bb, 0, 0)),
    ]
    vm = 2 * (tm + 2 * halo) * d * 4 + 2 * tm * d * 4 + 2 * tm * d * 4 + 8 * tm * d * 4 + 4 * w_a.size * 2
    return pl.pallas_call(
        functools.partial(_lru_body, nt=nt, reverse=reverse),
        out_shape=(jax.ShapeDtypeStruct((b, t, d), F32), jax.ShapeDtypeStruct((b, 1, d), F32)),
        grid=(b, nt),
        in_specs=in_specs,
        out_specs=(pl.BlockSpec((None, tm, d), lambda bb, i: (bb, tmap(i), 0)),
                   pl.BlockSpec((None, 1, d), lambda bb, i: (bb, 0, 0))),
        scratch_shapes=[pltpu.VMEM((1, d), F32), pltpu.VMEM((tm, d), F32), pltpu.VMEM((tm, d), F32)],
        compiler_params=_params(("arbitrary", "arbitrary"), vm),
        name="rglru_bwd" if reverse else "rglru_fwd",
    )(gu, gu, gu, conv_w, conv_b.reshape(1, d), w_a, b_a.reshape(1, d), w_x, b_x.reshape(1, d),
      lam.reshape(1, d), h0)


def _peer_query_body(x_ref, g_ref, sh_ref, sc_ref, wq_ref, keys_ref, ht_ref, st_ref):
    h = _norm_mod(x_ref[...], g_ref[...], sh_ref[...], sc_ref[...])
    hb = h.astype(BF16)
    ht_ref[...] = h.T.astype(BF16)
    q = jnp.dot(hb, wq_ref[...], preferred_element_type=F32).astype(BF16)
    for hp in range(2 * PEER_HEADS):
        qc = q[:, hp * PEER_KEY_DIM:(hp + 1) * PEER_KEY_DIM]
        st_ref[hp] = _dot_nt(keys_ref[hp % 2], qc)


def _peer_query(x, gain, shift, scale, wq, keys):
    b, t, d = x.shape
    tm = _tile(t, 512)
    nt = t // tm
    nq = wq.shape[1]
    vm = 2 * tm * d * 4 + 2 * d * nq * 2 + 2 * d * tm * 2 + 2 * 2 * PEER_HEADS * N_KEYS * tm * 4 + 4 * tm * d * 4
    return pl.pallas_call(
        _peer_query_body,
        out_shape=(jax.ShapeDtypeStruct((d, b * t), BF16),
                   jax.ShapeDtypeStruct((2 * PEER_HEADS, N_KEYS, b * t), F32)),
        grid=(b, nt),
        in_specs=[
            pl.BlockSpec((None, tm, d), lambda bb, i: (bb, i, 0)),
            pl.BlockSpec((1, d), lambda bb, i: (0, 0)),
            pl.BlockSpec((None, 1, d), lambda bb, i: (bb, 0, 0)),
            pl.BlockSpec((None, 1, d), lambda bb, i: (bb, 0, 0)),
            pl.BlockSpec((d, nq), lambda bb, i: (0, 0)),
            pl.BlockSpec(keys.shape, lambda bb, i: (0, 0, 0)),
        ],
        out_specs=(pl.BlockSpec((d, tm), lambda bb, i: (0, bb * nt + i)),
                   pl.BlockSpec((2 * PEER_HEADS, N_KEYS, tm), lambda bb, i: (0, 0, bb * nt + i))),
        compiler_params=_params(("arbitrary", "arbitrary"), vm),
        name="peer_query",
    )(x, gain.reshape(1, d), shift, scale, wq, keys)


def _oddeven_merge(lo, hi, r):
    step = r * 2
    if step < hi - lo:
        yield from _oddeven_merge(lo, hi, step)
        yield from _oddeven_merge(lo + r, hi, step)
        yield from [(k, k + r) for k in range(lo + r, hi - r, step)]
    else:
        yield (lo, lo + r)


def _oddeven_sort(lo, hi):
    if hi - lo >= 1:
        mid = lo + (hi - lo) // 2
        yield from _oddeven_sort(lo, mid)
        yield from _oddeven_sort(mid + 1, hi)
        yield from _oddeven_merge(lo, hi, 1)


_SORT16 = tuple(_oddeven_sort(0, PEER_TOPK - 1))


def _exchange(x, p, q):
    hi = jnp.maximum(x[p], x[q])
    lo = jnp.minimum(x[p], x[q])
    x[p] = hi
    x[q] = lo


def _merge_sublanes(x):
    n = len(x)
    shift = SUBLANES // 2
    while shift >= 1:
        z = [jnp.maximum(x[k], pltpu.roll(x[n - 1 - k], shift, 0)) for k in range(n)]
        dist = n // 2
        while dist >= 1:
            for k in range(n):
                if k & dist == 0:
                    _exchange(z, k, k + dist)
            dist //= 2
        x = z
        shift //= 2
    return x


def _top16_sorted(s):
    x = [s[SUBLANES * v:SUBLANES * (v + 1)] for v in range(s.shape[0] // SUBLANES)]
    assert len(x) == PEER_TOPK
    for p, q in _SORT16:
        _exchange(x, p, q)
    return _merge_sublanes(x)


def _route_stats(st_ref, e1_scr, e2_scr, tau_scr):
    tm = st_ref.shape[2]
    row = lax.broadcasted_iota(jnp.int32, (SUBLANES, tm), 0)
    for h in range(PEER_HEADS):
        s1 = st_ref[2 * h]
        s2 = st_ref[2 * h + 1]
        a = _top16_sorted(s1)
        b = _top16_sorted(s2)
        a_lo = a[SUBLANES - 1]
        a_hi = a[2 * SUBLANES - 1]
        for i in range(SUBLANES - 2, -1, -1):
            a_lo = jnp.where(row == i, a[i], a_lo)
            a_hi = jnp.where(row == i, a[SUBLANES + i], a_hi)
        c = [a_lo + b[j] for j in range(PEER_TOPK)]
        d = a_hi + b[0]
        c = [jnp.maximum(c[0], d)] + [
            jnp.maximum(c[j], jnp.minimum(c[j - 1], d)) for j in range(1, PEER_TOPK)]
        t = _merge_sublanes(c)
        z = jnp.exp(t[0] - t[0])
        for k in range(1, PEER_TOPK):
            z = z + jnp.exp(t[k] - t[0])
        inv_z = 1.0 / z
        tau_scr[h] = t[PEER_TOPK - 1]
        e1_scr[h] = jnp.exp(s1 - a[0][0:1]) * inv_z[0:1]
        e2_scr[h] = jnp.exp(s2 - b[0][0:1])


def _peer_expert_body(*refs, eb, ne, final_norm):
    if final_norm:
        (ht_ref, st_ref, u_ref, vt_ref, xs_ref, gate_ref, gain_ref, o_ref,
         e1_scr, e2_scr, tau_scr, acc_scr) = refs
    else:
        (ht_ref, st_ref, u_ref, vt_ref, xs_ref, gate_ref, o_ref,
         e1_scr, e2_scr, tau_scr, acc_scr) = refs
    j = pl.program_id(2)

    @pl.when(j == 0)
    def _():
        _route_stats(st_ref, e1_scr, e2_scr, tau_scr)
        acc_scr[...] = jnp.zeros_like(acc_scr)

    zt = jnp.dot(u_ref[...], ht_ref[...], preferred_element_type=F32)
    acts = []
    for l in range(eb // N_KEYS):
        i1 = j * (eb // N_KEYS) + l
        g = None
        for h in range(PEER_HEADS):
            s1 = st_ref[2 * h, pl.ds(i1, 1), :]
            e1 = e1_scr[h, pl.ds(i1, 1), :]
            tau = tau_scr[h, 0:1, :]
            gh = jnp.where(st_ref[2 * h + 1] + s1 >= tau, e2_scr[h] * e1, 0.0)
            g = gh if g is None else g + gh
        z = zt[l * N_KEYS:(l + 1) * N_KEYS]
        acts.append((jax.nn.gelu(z) * g).astype(BF16))
    act = jnp.concatenate(acts, axis=0)
    acc_scr[...] += jnp.dot(vt_ref[...], act, preferred_element_type=F32)

    @pl.when(j == ne - 1)
    def _():
        y = xs_ref[...] + gate_ref[...] * acc_scr[...].T
        if final_norm:
            ms = jnp.mean(y * y, axis=-1, keepdims=True)
            y = (y * lax.rsqrt(ms + NORM_EPS)) * gain_ref[...]
        o_ref[...] = y


def _peer_experts(ht, st, u, vt, xs, gate, final_gain=None):
    b, t, d = xs.shape
    e = u.shape[0]
    tm = _tile(t, 512)
    nt = t // tm
    eb = 512
    ne = e // eb
    final_norm = final_gain is not None
    in_specs = [
        pl.BlockSpec((d, tm), lambda bb, i, j: (0, bb * nt + i)),
        pl.BlockSpec((2 * PEER_HEADS, N_KEYS, tm), lambda bb, i, j: (0, 0, bb * nt + i)),
        pl.BlockSpec((eb, d), lambda bb, i, j: (j, 0)),
        pl.BlockSpec((d, eb), lambda bb, i, j: (0, j)),
        pl.BlockSpec((None, tm, d), lambda bb, i, j: (bb, i, 0)),
        pl.BlockSpec((None, 1, d), lambda bb, i, j: (bb, 0, 0)),
    ]
    args = [ht, st, u, vt, xs, gate]
    if final_norm:
        in_specs.append(pl.BlockSpec((1, d), lambda bb, i, j: (0, 0)))
        args.append(final_gain.reshape(1, d))
    stat = PEER_HEADS * N_KEYS * tm * 4
    vm = (2 * d * tm * 2 + 2 * 2 * stat + 4 * eb * d * 2 + 4 * tm * d * 4
          + 2 * stat + PEER_HEADS * SUBLANES * tm * 4 + d * tm * 4 + 2 * eb * tm * 4)
    return pl.pallas_call(
        functools.partial(_peer_expert_body, eb=eb, ne=ne, final_norm=final_norm),
        out_shape=jax.ShapeDtypeStruct((b, t, d), F32),
        grid=(b, nt, ne),
        in_specs=in_specs,
        out_specs=pl.BlockSpec((None, tm, d), lambda bb, i, j: (bb, i, 0)),
        scratch_shapes=[
            pltpu.VMEM((PEER_HEADS, N_KEYS, tm), F32),
            pltpu.VMEM((PEER_HEADS, N_KEYS, tm), F32),
            pltpu.VMEM((PEER_HEADS, SUBLANES, tm), F32),
            pltpu.VMEM((d, tm), F32),
        ],
        compiler_params=_params(("arbitrary", "arbitrary", "arbitrary"), vm),
        name="peer_experts",
    )(*args)


def _peer(xs, gain, shift, scale, gate, wq, keys, u, vt, final_gain=None):
    ht, st = _peer_query(xs, gain, shift, scale, wq, keys)
    return _peer_experts(ht, st, u, vt, xs, gate, final_gain)


def _rope_tables(s):
    t = jnp.arange(s)
    row = (t // GRID_W).astype(F32)
    col = (t % GRID_W).astype(F32)
    half = HEAD_DIM // 2
    inv = ROPE_BASE ** (-jnp.arange(0, half, 2, dtype=F32) / half)
    ang_r = row[:, None] * inv[None, :]
    ang_c = col[:, None] * inv[None, :]
    ang = jnp.concatenate([ang_r, ang_r, ang_c, ang_c], axis=-1)
    lane = jnp.arange(HEAD_DIM)
    sign = jnp.where((lane % half) < half // 2, -1.0, 1.0).astype(F32)
    return jnp.cos(ang), jnp.sin(ang) * sign[None, :]


def kernel(x, c, ctx, c_ctx, w_mod, b_mod, norm_mix, norm_ffn, norm_final, attn_w_qkv, attn_w_o, attn_sink, rec_w_in, rec_conv_w, rec_conv_b, rec_w_a, rec_b_a, rec_w_x, rec_b_x, rec_lambda, rec_w_out, peer_w_q, peer_keys, peer_u, peer_v):
    b, s, d = x.shape
    cl = ctx.shape[1]
    depth = w_mod.shape[0]
    rows = -(-(b + 1) // SUBLANES) * SUBLANES
    cvec = jnp.zeros((rows, d), F32).at[:b].set(c).at[b].set(c_ctx)
    mod = _mod_vectors(cvec, w_mod, b_mod).reshape(depth, rows, N_MOD, d)
    cos, sin_signed = _rope_tables(s)
    q_cols = N_Q_HEADS * HEAD_DIM
    kv_cols = N_KV_HEADS * HEAD_DIM

    xs = x
    cs = ctx.reshape(1, b * cl, d)
    for i in range(depth):
        last = i == depth - 1
        j = i // 2
        mx = [mod[i, :b, k][:, None, :] for k in range(N_MOD)]
        mc = [mod[i, b:b + 1, k][:, None, :] for k in range(N_MOD)]

        if i % 2 == 0:
            wqkv = attn_w_qkv[j].astype(BF16)
            wo = attn_w_o[j].astype(BF16)
            qkv_x = _proj(xs, norm_mix[i], mx[0], mx[1], wqkv, BF16, 512,
                          rope=(cos, sin_signed, q_cols + kv_cols))
            qkv_c = _proj(cs, norm_mix[i], mc[0], mc[1], wqkv, BF16, 512).reshape(b, cl, -1)
            o_x = _attention(qkv_x, qkv_c, attn_sink[j])
            xs = _oproj(o_x, wo, xs, mx[2])
            if not last:
                o_c = _ctx_attention(qkv_c, attn_sink[j]).reshape(1, b * cl, q_cols)
                cs = _oproj(o_c, wo, cs, mc[2])
        else:
            w_in = rec_w_in[j].astype(BF16)
            w_out = rec_w_out[j].astype(BF16)
            w_a = rec_w_a[j].astype(BF16)
            w_x = rec_w_x[j].astype(BF16)
            gu_x = _proj(xs, norm_mix[i], mx[0], mx[1], w_in, F32, 1024)
            gu_c = _proj(cs, norm_mix[i], mc[0], mc[1], w_in, F32, 1024).reshape(b, cl, -1)
            h0 = jnp.zeros((b, 1, d), F32)
            ys_c, ys_x = [], []
            for r in range(2):
                lru = functools.partial(
                    _lru_scan, conv_w=rec_conv_w[j], conv_b=rec_conv_b[j], w_a=w_a[r],
                    b_a=rec_b_a[j, r], w_x=w_x[r], b_x=rec_b_x[j, r], lam=rec_lambda[j, r],
                    reverse=(r == 1))
                y_c, h_c = lru(gu_c, h0=h0)
                y_x, _ = lru(gu_x, h0=h_c)
                ys_c.append(y_c)
                ys_x.append(y_x)
            xs = _rec_oproj(ys_x[0], ys_x[1], gu_x, w_out, xs, mx[2])
            if not last:
                cs = _rec_oproj(ys_c[0].reshape(1, b * cl, d), ys_c[1].reshape(1, b * cl, d),
                                gu_c.reshape(1, b * cl, -1), w_out, cs, mc[2])

        wq = peer_w_q[i].astype(BF16)
        keys = peer_keys[i].astype(BF16)
        u = peer_u[i].astype(BF16)
        vt = peer_v[i].T.astype(BF16)
        xs = _peer(xs, norm_ffn[i], mx[3], mx[4], mx[5], wq, keys, u, vt,
                   final_gain=norm_final if last else None)
        if not last:
            cs = _peer(cs, norm_ffn[i], mc[3], mc[4], mc[5], wq, keys, u, vt)
    return xs
```

```python
import functools
import math

import jax
import jax.numpy as jnp
from jax import lax
from jax.experimental import pallas as pl
from jax.experimental.pallas import tpu as pltpu

F32 = jnp.float32
BF16 = jnp.bfloat16

NORM_EPS = 1e-6
N_MOD = 6
GRID_W = 64
HEAD_DIM = 128
N_Q_HEADS = 16
N_KV_HEADS = 4
GQA_GROUP = N_Q_HEADS // N_KV_HEADS
ATTN_BLOCK = 128
ROPE_BASE = 10000.0
RNN_BLOCKS = 8
CONV_W = 4
CONV_LEFT = 1
LRU_C = 8.0
PEER_HEADS = 8
PEER_KEY_DIM = 128
N_KEYS = 128
PEER_TOPK = 16

V7X_VMEM_BYTES = 64 * 1024 * 1024
SUBLANES = 8
LANES = 128
VMEM_CAP = V7X_VMEM_BYTES - 6 * 1024 * 1024


def _params(sem, vmem_bytes):
    limit = int(min(VMEM_CAP, max(32 * 1024 * 1024, vmem_bytes * 5 // 4)))
    return pltpu.CompilerParams(dimension_semantics=sem, vmem_limit_bytes=limit)


def _tile(n, pref):
    return pref if n % pref == 0 else n


def _norm_mod(x, gain, shift, scale):
    ms = jnp.mean(x * x, axis=-1, keepdims=True)
    y = x * lax.rsqrt(ms + NORM_EPS)
    return (y * gain) * (1.0 + scale) + shift


def _dot_nt(a, b):
    return lax.dot_general(a, b, (((1,), (1,)), ((), ())), preferred_element_type=F32)


def _mod_body(c_ref, w_ref, b_ref, o_ref):
    s = jax.nn.silu(c_ref[...])
    o_ref[0] = jnp.dot(s.astype(BF16), w_ref[0].astype(BF16), preferred_element_type=F32) + b_ref[0]


def _mod_vectors(cvec, w_mod, b_mod):
    depth, d, n = w_mod.shape
    r = cvec.shape[0]
    tn = _tile(n, 1536)
    vm = 2 * d * tn * 4 + d * tn * 2 + 4 * r * tn * 4
    return pl.pallas_call(
        _mod_body,
        out_shape=jax.ShapeDtypeStruct((depth, r, n), F32),
        grid=(depth, n // tn),
        in_specs=[
            pl.BlockSpec((r, d), lambda i, j: (0, 0)),
            pl.BlockSpec((1, d, tn), lambda i, j: (i, 0, j)),
            pl.BlockSpec((1, 1, tn), lambda i, j: (i, 0, j)),
        ],
        out_specs=pl.BlockSpec((1, r, tn), lambda i, j: (i, 0, j)),
        compiler_params=_params(("arbitrary", "arbitrary"), vm),
        name="mod_vectors",
    )(cvec, w_mod, b_mod.reshape(depth, 1, n))


def _rope(x, cos, sin_signed):
    lane = lax.broadcasted_iota(jnp.int32, x.shape, 1)
    qtr = HEAD_DIM // 4
    first = (lane % (2 * qtr)) < qtr
    rot = jnp.where(first, pltpu.roll(x, HEAD_DIM - qtr, 1), pltpu.roll(x, qtr, 1))
    return x * cos + rot * sin_signed


def _proj_body(*refs, rope_blocks):
    if rope_blocks:
        x_ref, g_ref, sh_ref, sc_ref, w_ref, cos_ref, sin_ref, o_ref, h_scr = refs
    else:
        x_ref, g_ref, sh_ref, sc_ref, w_ref, o_ref, h_scr = refs
    j = pl.program_id(2)

    @pl.when(j == 0)
    def _():
        h_scr[...] = _norm_mod(x_ref[...], g_ref[...], sh_ref[...], sc_ref[...]).astype(BF16)

    y = jnp.dot(h_scr[...], w_ref[...], preferred_element_type=F32)
    if rope_blocks:

        @pl.when(j < rope_blocks)
        def _():
            cos = cos_ref[...]
            sin = sin_ref[...]
            parts = [
                _rope(y[:, k * HEAD_DIM:(k + 1) * HEAD_DIM], cos, sin)
                for k in range(y.shape[1] // HEAD_DIM)
            ]
            o_ref[...] = jnp.concatenate(parts, axis=1).astype(o_ref.dtype)

        @pl.when(j >= rope_blocks)
        def _():
            o_ref[...] = y.astype(o_ref.dtype)
    else:
        o_ref[...] = y.astype(o_ref.dtype)


def _proj(x, gain, shift, scale, w, out_dtype, tn, rope=None):
    b, t, d = x.shape
    n = w.shape[1]
    tm = _tile(t, 512)
    in_specs = [
        pl.BlockSpec((None, tm, d), lambda bb, i, j: (bb, i, 0)),
        pl.BlockSpec((1, d), lambda bb, i, j: (0, 0)),
        pl.BlockSpec((None, 1, d), lambda bb, i, j: (bb, 0, 0)),
        pl.BlockSpec((None, 1, d), lambda bb, i, j: (bb, 0, 0)),
        pl.BlockSpec((d, tn), lambda bb, i, j: (0, j)),
    ]
    args = [x, gain.reshape(1, d), shift, scale, w]
    rope_blocks = 0
    if rope is not None:
        cos, sin_signed, n_cols = rope
        assert n_cols % tn == 0
        rope_blocks = n_cols // tn
        in_specs += [pl.BlockSpec((tm, HEAD_DIM), lambda bb, i, j: (i, 0))] * 2
        args += [cos, sin_signed]
    osz = jnp.dtype(out_dtype).itemsize
    vm = 2 * tm * d * 4 + 2 * d * tn * 2 + 2 * tm * tn * osz + tm * d * 2 + 3 * tm * tn * 4
    return pl.pallas_call(
        functools.partial(_proj_body, rope_blocks=rope_blocks),
        out_shape=jax.ShapeDtypeStruct((b, t, n), out_dtype),
        grid=(b, t // tm, n // tn),
        in_specs=in_specs,
        out_specs=pl.BlockSpec((None, tm, tn), lambda bb, i, j: (bb, i, j)),
        scratch_shapes=[pltpu.VMEM((tm, d), BF16)],
        compiler_params=_params(("arbitrary", "arbitrary", "arbitrary"), vm),
        name="norm_mod_proj",
    )(*args)


def _oproj_body(a_ref, w_ref, res_ref, gate_ref, o_ref):
    y = jnp.dot(a_ref[...], w_ref[...], preferred_element_type=F32)
    o_ref[...] = res_ref[...] + gate_ref[...] * y


def _rec_oproj_body(yf_ref, yb_ref, gu_ref, w_ref, res_ref, gate_ref, o_ref):
    a = (yf_ref[...] + yb_ref[...]) * jax.nn.gelu(gu_ref[...])
    y = jnp.dot(a.astype(BF16), w_ref[...], preferred_element_type=F32)
    o_ref[...] = res_ref[...] + gate_ref[...] * y


def _oproj(a, w, res, gate):
    b, t, k = a.shape
    d = w.shape[1]
    tm = _tile(t, 512)
    vm = 2 * tm * k * 2 + 2 * k * d * 2 + 4 * tm * d * 4 + tm * d * 4
    return pl.pallas_call(
        _oproj_body,
        out_shape=jax.ShapeDtypeStruct((b, t, d), F32),
        grid=(b, t // tm),
        in_specs=[
            pl.BlockSpec((None, tm, k), lambda bb, i: (bb, i, 0)),
            pl.BlockSpec((k, d), lambda bb, i: (0, 0)),
            pl.BlockSpec((None, tm, d), lambda bb, i: (bb, i, 0)),
            pl.BlockSpec((None, 1, d), lambda bb, i: (bb, 0, 0)),
        ],
        out_specs=pl.BlockSpec((None, tm, d), lambda bb, i: (bb, i, 0)),
        compiler_params=_params(("arbitrary", "arbitrary"), vm),
        name="oproj_residual",
    )(a, w, res, gate)


def _rec_oproj(yf, yb, gu, w, res, gate):
    b, t, k = yf.shape
    d = w.shape[1]
    tm = _tile(t, 256)
    vm = 6 * tm * k * 4 + 2 * k * d * 2 + 4 * tm * d * 4 + 3 * tm * d * 4
    return pl.pallas_call(
        _rec_oproj_body,
        out_shape=jax.ShapeDtypeStruct((b, t, d), F32),
        grid=(b, t // tm),
        in_specs=[
            pl.BlockSpec((None, tm, k), lambda bb, i: (bb, i, 0)),
            pl.BlockSpec((None, tm, k), lambda bb, i: (bb, i, 0)),
            pl.BlockSpec((None, tm, k), lambda bb, i: (bb, i, 0)),
            pl.BlockSpec((k, d), lambda bb, i: (0, 0)),
            pl.BlockSpec((None, tm, d), lambda bb, i: (bb, i, 0)),
            pl.BlockSpec((None, 1, d), lambda bb, i: (bb, 0, 0)),
        ],
        out_specs=pl.BlockSpec((None, tm, d), lambda bb, i: (bb, i, 0)),
        compiler_params=_params(("arbitrary", "arbitrary"), vm),
        name="rec_oproj_residual",
    )(yf, yb, gu, w, res, gate)


def _stack_groups(q):
    return jnp.concatenate(
        [q[:, g * HEAD_DIM:(g + 1) * HEAD_DIM] for g in range(GQA_GROUP)], axis=0)


def _unstack_groups(o, rows):
    return jnp.concatenate([o[g * rows:(g + 1) * rows] for g in range(GQA_GROUP)], axis=1)


def _sink_column(sink_ref, h, rows):
    return jnp.concatenate(
        [jnp.full((rows, 1), sink_ref[h * GQA_GROUP + g], F32) for g in range(GQA_GROUP)], axis=0)


def _attn_body(sink_ref, q_ref, kp_ref, kc_ref, kn_ref, vp_ref, vc_ref, vn_ref, kx_ref, vx_ref,
               o_ref, *, nb):
    i = pl.program_id(1)
    h = pl.program_id(2)
    blk = ATTN_BLOCK
    scale = HEAD_DIM ** -0.5
    qs = _stack_groups(q_ref[...])
    r = lax.broadcasted_iota(jnp.int32, (GQA_GROUP * blk, blk), 0) % blk
    c = lax.broadcasted_iota(jnp.int32, (GQA_GROUP * blk, blk), 1)
    neg = -jnp.inf
    s_p = jnp.where((c >= r) & (i >= 1), _dot_nt(qs, kp_ref[...]) * scale, neg)
    s_c = _dot_nt(qs, kc_ref[...]) * scale
    s_n = jnp.where((c <= r) & (i + 1 < nb), _dot_nt(qs, kn_ref[...]) * scale, neg)
    s_x = _dot_nt(qs, kx_ref[...]) * scale
    sink = _sink_column(sink_ref, h, blk)
    m = jnp.maximum(
        jnp.maximum(jnp.maximum(s_p.max(-1, keepdims=True), s_c.max(-1, keepdims=True)),
                    jnp.maximum(s_n.max(-1, keepdims=True), s_x.max(-1, keepdims=True))),
        sink)
    p_p = jnp.exp(s_p - m)
    p_c = jnp.exp(s_c - m)
    p_n = jnp.exp(s_n - m)
    p_x = jnp.exp(s_x - m)
    denom = (p_p.sum(-1, keepdims=True) + p_c.sum(-1, keepdims=True)
             + p_n.sum(-1, keepdims=True) + p_x.sum(-1, keepdims=True) + jnp.exp(sink - m))
    o = (jnp.dot(p_p.astype(BF16), vp_ref[...], preferred_element_type=F32)
         + jnp.dot(p_c.astype(BF16), vc_ref[...], preferred_element_type=F32)
         + jnp.dot(p_n.astype(BF16), vn_ref[...], preferred_element_type=F32)
         + jnp.dot(p_x.astype(BF16), vx_ref[...], preferred_element_type=F32))
    o_ref[...] = _unstack_groups(o / denom, blk).astype(o_ref.dtype)


def _attention(qkv_x, qkv_c, sink):
    b, s, _ = qkv_x.shape
    c = qkv_c.shape[1]
    nb = s // ATTN_BLOCK
    qw = GQA_GROUP * HEAD_DIM
    k0 = N_Q_HEADS
    v0 = N_Q_HEADS + N_KV_HEADS
    blk = ATTN_BLOCK

    def kv_spec(col0, off):
        def idx(bb, i, h):
            return (bb, jnp.clip(i + off, 0, nb - 1), col0 + h)
        return pl.BlockSpec((None, blk, HEAD_DIM), idx)

    in_specs = [
        pl.BlockSpec(memory_space=pltpu.SMEM),
        pl.BlockSpec((None, blk, qw), lambda bb, i, h: (bb, i, h)),
        kv_spec(k0, -1), kv_spec(k0, 0), kv_spec(k0, 1),
        kv_spec(v0, -1), kv_spec(v0, 0), kv_spec(v0, 1),
        pl.BlockSpec((None, c, HEAD_DIM), lambda bb, i, h: (bb, 0, k0 + h)),
        pl.BlockSpec((None, c, HEAD_DIM), lambda bb, i, h: (bb, 0, v0 + h)),
    ]
    return pl.pallas_call(
        functools.partial(_attn_body, nb=nb),
        out_shape=jax.ShapeDtypeStruct((b, s, N_Q_HEADS * HEAD_DIM), BF16),
        grid=(b, nb, N_KV_HEADS),
        in_specs=in_specs,
        out_specs=pl.BlockSpec((None, blk, qw), lambda bb, i, h: (bb, i, h)),
        compiler_params=_params(("arbitrary", "arbitrary", "arbitrary"), 16 * 1024 * 1024),
        name="window_attention",
    )(sink, qkv_x, qkv_x, qkv_x, qkv_x, qkv_x, qkv_x, qkv_x, qkv_c, qkv_c)


def _ctx_attn_body(sink_ref, q_ref, k_ref, v_ref, o_ref):
    h = pl.program_id(1)
    rows = q_ref.shape[0]
    scale = HEAD_DIM ** -0.5
    qs = _stack_groups(q_ref[...])
    s = _dot_nt(qs, k_ref[...]) * scale
    sink = _sink_column(sink_ref, h, rows)
    m = jnp.maximum(s.max(-1, keepdims=True), sink)
    p = jnp.exp(s - m)
    denom = p.sum(-1, keepdims=True) + jnp.exp(sink - m)
    o = jnp.dot((p / denom).astype(BF16), v_ref[...], preferred_element_type=F32)
    o_ref[...] = _unstack_groups(o, rows).astype(o_ref.dtype)


def _ctx_attention(qkv_c, sink):
    b, c, _ = qkv_c.shape
    qw = GQA_GROUP * HEAD_DIM
    k0 = N_Q_HEADS
    v0 = N_Q_HEADS + N_KV_HEADS
    return pl.pallas_call(
        _ctx_attn_body,
        out_shape=jax.ShapeDtypeStruct((b, c, N_Q_HEADS * HEAD_DIM), BF16),
        grid=(b, N_KV_HEADS),
        in_specs=[
            pl.BlockSpec(memory_space=pltpu.SMEM),
            pl.BlockSpec((None, c, qw), lambda bb, h: (bb, 0, h)),
            pl.BlockSpec((None, c, HEAD_DIM), lambda bb, h: (bb, 0, k0 + h)),
            pl.BlockSpec((None, c, HEAD_DIM), lambda bb, h: (bb, 0, v0 + h)),
        ],
        out_specs=pl.BlockSpec((None, c, qw), lambda bb, h: (bb, 0, h)),
        compiler_params=_params(("arbitrary", "arbitrary"), 16 * 1024 * 1024),
        name="context_attention",
    )(sink, qkv_c, qkv_c, qkv_c)


def _block_diag(ub, w_ref, bias):
    bw = w_ref.shape[1]
    return jnp.concatenate(
        [jnp.dot(ub[:, n * bw:(n + 1) * bw], w_ref[n], preferred_element_type=F32)
         for n in range(w_ref.shape[0])], axis=1) + bias


def _lru_body(up_ref, uc_ref, un_ref, cw_ref, cb_ref, wa_ref, ba_ref, wx_ref, bx_ref, lam_ref,
              h0_ref, y_ref, hl_ref, h_scr, a_scr, b_scr, *, nt, reverse):
    i = pl.program_id(1)
    ti = (nt - 1 - i) if reverse else i
    tm = uc_ref.shape[0]
    halo = up_ref.shape[0]

    @pl.when(i == 0)
    def _():
        h_scr[...] = h0_ref[...]

    prev = jnp.where(ti > 0, up_ref[...], 0.0)
    nxt = jnp.where(ti < nt - 1, un_ref[...], 0.0)
    ext = jnp.concatenate([prev, uc_ref[...], nxt], axis=0)
    u = cb_ref[...]
    for k in range(CONV_W):
        off = halo - CONV_LEFT + k
        u = u + ext[off:off + tm] * cw_ref[k:k + 1, :]

    ub = u.astype(BF16)
    r = jax.nn.sigmoid(_block_diag(ub, wa_ref, ba_ref[...]))
    ig = jax.nn.sigmoid(_block_diag(ub, wx_ref, bx_ref[...]))
    nl = -lam_ref[...]
    softplus = jnp.maximum(nl, 0.0) + jnp.log1p(jnp.exp(-jnp.abs(nl)))
    log_a = -LRU_C * r * softplus
    a_scr[...] = jnp.exp(log_a)
    b_scr[...] = jnp.sqrt(1.0 - jnp.exp(2.0 * log_a)) * ig * u

    ng = tm // SUBLANES
    row = lax.broadcasted_iota(jnp.int32, (SUBLANES, a_scr.shape[1]), 0)

    def group(g, h):
        gi = (ng - 1 - g) if reverse else g
        r0 = pl.multiple_of(gi * SUBLANES, SUBLANES)
        a = a_scr[pl.ds(r0, SUBLANES), :]
        bb = b_scr[pl.ds(r0, SUBLANES), :]
        for k in (1, 2, 4):
            if reverse:
                keep = row < SUBLANES - k
                shift = SUBLANES - k
            else:
                keep = row >= k
                shift = k
            a_sh = pltpu.roll(a, shift, 0)
            b_sh = pltpu.roll(bb, shift, 0)
            bb = bb + a * jnp.where(keep, b_sh, 0.0)
            a = a * jnp.where(keep, a_sh, 1.0)
        y = bb + a * h
        y_ref[pl.ds(r0, SUBLANES), :] = y
        return y[0:1, :] if reverse else y[SUBLANES - 1:SUBLANES, :]

    h_last = lax.fori_loop(0, ng, group, h_scr[...])
    h_scr[...] = h_last

    @pl.when(i == nt - 1)
    def _():
        hl_ref[...] = h_last


def _lru_scan(gu, conv_w, conv_b, w_a, b_a, w_x, b_x, lam, h0, reverse):
    b, t, d2 = gu.shape
    d = d2 // 2
    tm = _tile(t, 256)
    nt = t // tm
    halo = SUBLANES
    hb = tm // halo
    nh = t // halo

    def tmap(i):
        return (nt - 1 - i) if reverse else i

    vec = lambda: pl.BlockSpec((1, d), lambda bb, i: (0, 0))
    in_specs = [
        pl.BlockSpec((None, halo, d), lambda bb, i: (bb, jnp.maximum(tmap(i) * hb - 1, 0), 1)),
        pl.BlockSpec((None, tm, d), lambda bb, i: (bb, tmap(i), 1)),
        pl.BlockSpec((None, halo, d), lambda bb, i: (bb, jnp.minimum((tmap(i) + 1) * hb, nh - 1), 1)),
        pl.BlockSpec((CONV_W, d), lambda bb, i: (0, 0)),
        vec(),
        pl.BlockSpec(w_a.shape, lambda bb, i: (0, 0, 0)),
        vec(),
        pl.BlockSpec(w_x.shape, lambda bb, i: (0, 0, 0)),
        vec(),
        vec(),
        pl.BlockSpec((None, 1, d), lambda bb, i: (bb, 0, 0)),
    ]
    vm = 2 * (tm + 2 * halo) * d * 4 + 2 * tm * d * 4 + 2 * tm * d * 4 + 8 * tm * d * 4 + 4 * w_a.size * 2
    return pl.pallas_call(
        functools.partial(_lru_body, nt=nt, reverse=reverse),
        out_shape=(jax.ShapeDtypeStruct((b, t, d), F32), jax.ShapeDtypeStruct((b, 1, d), F32)),
        grid=(b, nt),
        in_specs=in_specs,
        out_specs=(pl.BlockSpec((None, tm, d), lambda bb, i: (bb, tmap(i), 0)),
                   pl.BlockSpec((None, 1, d), lambda bb, i: (bb, 0, 0))),
        scratch_shapes=[pltpu.VMEM((1, d), F32), pltpu.VMEM((tm, d), F32), pltpu.VMEM((tm, d), F32)],
        compiler_params=_params(("arbitrary", "arbitrary"), vm),
        name="rglru_bwd" if reverse else "rglru_fwd",
    )(gu, gu, gu, conv_w, conv_b.reshape(1, d), w_a, b_a.reshape(1, d), w_x, b_x.reshape(1, d),
      lam.reshape(1, d), h0)


def _peer_query_body(x_ref, g_ref, sh_ref, sc_ref, wq_ref, keys_ref, ht_ref, st_ref):
    h = _norm_mod(x_ref[...], g_ref[...], sh_ref[...], sc_ref[...])
    hb = h.astype(BF16)
    ht_ref[...] = h.T.astype(BF16)
    q = jnp.dot(hb, wq_ref[...], preferred_element_type=F32).astype(BF16)
    for hp in range(2 * PEER_HEADS):
        qc = q[:, hp * PEER_KEY_DIM:(hp + 1) * PEER_KEY_DIM]
        st_ref[hp] = _dot_nt(keys_ref[hp % 2], qc)


def _peer_query(x, gain, shift, scale, wq, keys):
    b, t, d = x.shape
    tm = _tile(t, 512)
    nt = t // tm
    nq = wq.shape[1]
    vm = 2 * tm * d * 4 + 2 * d * nq * 2 + 2 * d * tm * 2 + 2 * 2 * PEER_HEADS * N_KEYS * tm * 4 + 4 * tm * d * 4
    return pl.pallas_call(
        _peer_query_body,
        out_shape=(jax.ShapeDtypeStruct((d, b * t), BF16),
                   jax.ShapeDtypeStruct((2 * PEER_HEADS, N_KEYS, b * t), F32)),
        grid=(b, nt),
        in_specs=[
            pl.BlockSpec((None, tm, d), lambda bb, i: (bb, i, 0)),
            pl.BlockSpec((1, d), lambda bb, i: (0, 0)),
            pl.BlockSpec((None, 1, d), lambda bb, i: (bb, 0, 0)),
            pl.BlockSpec((None, 1, d), lambda bb, i: (bb, 0, 0)),
            pl.BlockSpec((d, nq), lambda bb, i: (0, 0)),
            pl.BlockSpec(keys.shape, lambda bb, i: (0, 0, 0)),
        ],
        out_specs=(pl.BlockSpec((d, tm), lambda bb, i: (0, bb * nt + i)),
                   pl.BlockSpec((2 * PEER_HEADS, N_KEYS, tm), lambda bb, i: (0, 0, bb * nt + i))),
        compiler_params=_params(("arbitrary", "arbitrary"), vm),
        name="peer_query",
    )(x, gain.reshape(1, d), shift, scale, wq, keys)


def _oddeven_merge(lo, hi, r):
    step = r * 2
    if step < hi - lo:
        yield from _oddeven_merge(lo, hi, step)
        yield from _oddeven_merge(lo + r, hi, step)
        yield from [(k, k + r) for k in range(lo + r, hi - r, step)]
    else:
        yield (lo, lo + r)


def _oddeven_sort(lo, hi):
    if hi - lo >= 1:
        mid = lo + (hi - lo) // 2
        yield from _oddeven_sort(lo, mid)
        yield from _oddeven_sort(mid + 1, hi)
        yield from _oddeven_merge(lo, hi, 1)


_SORT16 = tuple(_oddeven_sort(0, PEER_TOPK - 1))


def _exchange(x, p, q):
    hi = jnp.maximum(x[p], x[q])
    lo = jnp.minimum(x[p], x[q])
    x[p] = hi
    x[q] = lo


def _merge_sublanes(x):
    n = len(x)
    shift = SUBLANES // 2
    while shift >= 1:
        z = [jnp.maximum(x[k], pltpu.roll(x[n - 1 - k], shift, 0)) for k in range(n)]
        dist = n // 2
        while dist >= 1:
            for k in range(n):
                if k & dist == 0:
                    _exchange(z, k, k + dist)
            dist //= 2
        x = z
        shift //= 2
    return x


def _top16_sorted(s):
    x = [s[SUBLANES * v:SUBLANES * (v + 1)] for v in range(s.shape[0] // SUBLANES)]
    assert len(x) == PEER_TOPK
    for p, q in _SORT16:
        _exchange(x, p, q)
    return _merge_sublanes(x)


def _dup16(v):
    bits = pltpu.bitcast(v.astype(BF16).astype(F32), jnp.uint32) >> 16
    return bits | (bits << 16)


def _route_stats(st_ref, e1d_scr, c1d_scr, e2_scr, r2_scr):
    tm = st_ref.shape[2]
    row = lax.broadcasted_iota(jnp.int32, (SUBLANES, tm), 0)
    for h in range(PEER_HEADS):
        s1 = st_ref[2 * h]
        s2 = st_ref[2 * h + 1]
        a = _top16_sorted(s1)
        b = _top16_sorted(s2)
        a_lo = a[SUBLANES - 1]
        a_hi = a[2 * SUBLANES - 1]
        for i in range(SUBLANES - 2, -1, -1):
            a_lo = jnp.where(row == i, a[i], a_lo)
            a_hi = jnp.where(row == i, a[SUBLANES + i], a_hi)
        c = [a_lo + b[j] for j in range(PEER_TOPK)]
        d = a_hi + b[0]
        c = [jnp.maximum(c[0], d)] + [
            jnp.maximum(c[j], jnp.minimum(c[j - 1], d)) for j in range(1, PEER_TOPK)]
        t = _merge_sublanes(c)
        z = jnp.ones_like(t[0])
        for k in range(1, PEER_TOPK):
            z = z + jnp.exp(t[k] - t[0])
        inv_z = 1.0 / z
        tau = t[PEER_TOPK - 1][0:1]
        count1 = jnp.zeros_like(s1)
        rank2 = jnp.zeros_like(s2)
        for j in range(PEER_TOPK):
            bj = b[j][0:1]
            count1 = count1 + jnp.where(s1 + bj >= tau, 1.0, 0.0)
            rank2 = rank2 + jnp.where(bj > s2, 1.0, 0.0)
        e1 = jnp.exp(s1 - a[0][0:1]) * inv_z[0:1]
        e2 = jnp.exp(s2 - b[0][0:1])
        for cc in range(tm // LANES):
            cs = slice(cc * LANES, (cc + 1) * LANES)
            c1d_scr[h, cc] = _dup16(count1[:, cs])
            e1d_scr[h, cc] = _dup16(e1[:, cs])
            r2_scr[h, cc] = rank2[:, cs].astype(BF16)
            e2_scr[h, cc] = e2[:, cs].astype(BF16)


def _dense_act(zt_ref, act_ref, blk, cols, e1d_scr, c1d_scr, e2_scr, r2_scr):
    eb = zt_ref.shape[0]
    for l in range(eb // N_KEYS):
        i1 = jnp.clip(blk * (eb // N_KEYS) + l, 0, N_KEYS - 1)
        rs = slice(l * N_KEYS, (l + 1) * N_KEYS)
        for c in cols:
            cs = slice(c * LANES, (c + 1) * LANES)
            g = None
            for h in range(PEER_HEADS):
                cnt = jnp.broadcast_to(c1d_scr[h, c, pl.ds(i1, 1), :], (N_KEYS // 2, LANES))
                e1 = jnp.broadcast_to(e1d_scr[h, c, pl.ds(i1, 1), :], (N_KEYS // 2, LANES))
                gh = jnp.where(r2_scr[h, c] < pltpu.bitcast(cnt, BF16),
                               e2_scr[h, c] * pltpu.bitcast(e1, BF16), jnp.zeros((), BF16))
                g = gh if g is None else g + gh
            act_ref[rs, cs] = jax.nn.gelu(zt_ref[rs, cs]).astype(BF16) * g


def _peer_expert_body(*refs, eb, ns, final_norm):
    if final_norm:
        (ht_ref, st_ref, u_ref, vt_ref, xs_ref, gate_ref, gain_ref, o_ref,
         e1d_scr, c1d_scr, e2_scr, r2_scr, acc_scr, zt_scr, act_scr) = refs
    else:
        (ht_ref, st_ref, u_ref, vt_ref, xs_ref, gate_ref, o_ref,
         e1d_scr, c1d_scr, e2_scr, r2_scr, acc_scr, zt_scr, act_scr) = refs
    j = pl.program_id(2)

    @pl.when(j == 0)
    def _():
        _route_stats(st_ref, e1d_scr, c1d_scr, e2_scr, r2_scr)
        acc_scr[...] = jnp.zeros_like(acc_scr)
        zt_scr[...] = jnp.zeros_like(zt_scr)
        act_scr[...] = jnp.zeros_like(act_scr)

    stats = (e1d_scr, c1d_scr, e2_scr, r2_scr)
    tm = ht_ref.shape[1]
    hw = tm // 2
    halves = [(slice(0, hw), range(0, hw // LANES)),
              (slice(hw, tm), range(hw // LANES, tm // LANES))]
    for s in (0, 1):
        for ts, cols in halves:
            acc_scr[:, ts] += jnp.dot(vt_ref[:, s * eb:(s + 1) * eb], act_scr[s, :, ts],
                                      preferred_element_type=F32)
            _dense_act(zt_scr.at[1 - s], act_scr.at[1 - s], 2 * j - 1 + s, cols, *stats)
            zt_scr[s, :, ts] = jnp.dot(u_ref[s * eb:(s + 1) * eb, :], ht_ref[:, ts],
                                       preferred_element_type=F32)

    @pl.when(j == ns)
    def _():
        y = xs_ref[...] + gate_ref[...] * acc_scr[...].T
        if final_norm:
            ms = jnp.mean(y * y, axis=-1, keepdims=True)
            y = (y * lax.rsqrt(ms + NORM_EPS)) * gain_ref[...]
        o_ref[...] = y


def _peer_experts(ht, st, u, vt, xs, gate, final_gain=None):
    b, t, d = xs.shape
    e = u.shape[0]
    tm = _tile(t, 512)
    nt = t // tm
    nc = tm // LANES
    eb = 512
    ns = e // (2 * eb)
    final_norm = final_gain is not None
    once = pl.Buffered(1)
    in_specs = [
        pl.BlockSpec((d, tm), lambda bb, i, j: (0, bb * nt + i), pipeline_mode=once),
        pl.BlockSpec((2 * PEER_HEADS, N_KEYS, tm), lambda bb, i, j: (0, 0, bb * nt + i),
                     pipeline_mode=once),
        pl.BlockSpec((2 * eb, d), lambda bb, i, j: (jnp.minimum(j, ns - 1), 0)),
        pl.BlockSpec((d, 2 * eb), lambda bb, i, j: (0, jnp.maximum(j - 1, 0))),
        pl.BlockSpec((None, tm, d), lambda bb, i, j: (bb, i, 0), pipeline_mode=once),
        pl.BlockSpec((None, 1, d), lambda bb, i, j: (bb, 0, 0)),
    ]
    args = [ht, st, u, vt, xs, gate]
    if final_norm:
        in_specs.append(pl.BlockSpec((1, d), lambda bb, i, j: (0, 0)))
        args.append(final_gain.reshape(1, d))
    stat = PEER_HEADS * N_KEYS * tm
    vm = (d * tm * 2 + 2 * stat * 4 + 8 * eb * d * 2 + 3 * tm * d * 4
          + stat * (4 + 4 + 2 + 2) + d * tm * 4 + 2 * eb * tm * 6 + 2 * tm * d * 4)
    return pl.pallas_call(
        functools.partial(_peer_expert_body, eb=eb, ns=ns, final_norm=final_norm),
        out_shape=jax.ShapeDtypeStruct((b, t, d), F32),
        grid=(b, nt, ns + 1),
        in_specs=in_specs,
        out_specs=pl.BlockSpec((None, tm, d), lambda bb, i, j: (bb, i, 0)),
        scratch_shapes=[
            pltpu.VMEM((PEER_HEADS, nc, N_KEYS, LANES), jnp.uint32),
            pltpu.VMEM((PEER_HEADS, nc, N_KEYS, LANES), jnp.uint32),
            pltpu.VMEM((PEER_HEADS, nc, N_KEYS, LANES), BF16),
            pltpu.VMEM((PEER_HEADS, nc, N_KEYS, LANES), BF16),
            pltpu.VMEM((d, tm), F32),
            pltpu.VMEM((2, eb, tm), F32),
            pltpu.VMEM((2, eb, tm), BF16),
        ],
        compiler_params=_params(("arbitrary", "arbitrary", "arbitrary"), vm),
        name="peer_experts",
    )(*args)


def _peer(xs, gain, shift, scale, gate, wq, keys, u, vt, final_gain=None):
    ht, st = _peer_query(xs, gain, shift, scale, wq, keys)
    return _peer_experts(ht, st, u, vt, xs, gate, final_gain)


def _rope_tables(s):
    t = jnp.arange(s)
    row = (t // GRID_W).astype(F32)
    col = (t % GRID_W).astype(F32)
    half = HEAD_DIM // 2
    inv = ROPE_BASE ** (-jnp.arange(0, half, 2, dtype=F32) / half)
    ang_r = row[:, None] * inv[None, :]
    ang_c = col[:, None] * inv[None, :]
    ang = jnp.concatenate([ang_r, ang_r, ang_c, ang_c], axis=-1)
    lane = jnp.arange(HEAD_DIM)
    sign = jnp.where((lane % half) < half // 2, -1.0, 1.0).astype(F32)
    return jnp.cos(ang), jnp.sin(ang) * sign[None, :]


def kernel(x, c, ctx, c_ctx, w_mod, b_mod, norm_mix, norm_ffn, norm_final, attn_w_qkv, attn_w_o, attn_sink, rec_w_in, rec_conv_w, rec_conv_b, rec_w_a, rec_b_a, rec_w_x, rec_b_x, rec_lambda, rec_w_out, peer_w_q, peer_keys, peer_u, peer_v):
    b, s, d = x.shape
    cl = ctx.shape[1]
    depth = w_mod.shape[0]
    rows = -(-(b + 1) // SUBLANES) * SUBLANES
    cvec = jnp.zeros((rows, d), F32).at[:b].set(c).at[b].set(c_ctx)
    mod = _mod_vectors(cvec, w_mod, b_mod).reshape(depth, rows, N_MOD, d)
    cos, sin_signed = _rope_tables(s)
    q_cols = N_Q_HEADS * HEAD_DIM
    kv_cols = N_KV_HEADS * HEAD_DIM

    xs = x
    cs = ctx.reshape(1, b * cl, d)
    for i in range(depth):
        last = i == depth - 1
        j = i // 2
        mx = [mod[i, :b, k][:, None, :] for k in range(N_MOD)]
        mc = [mod[i, b:b + 1, k][:, None, :] for k in range(N_MOD)]

        if i % 2 == 0:
            wqkv = attn_w_qkv[j].astype(BF16)
            wo = attn_w_o[j].astype(BF16)
            qkv_x = _proj(xs, norm_mix[i], mx[0], mx[1], wqkv, BF16, 512,
                          rope=(cos, sin_signed, q_cols + kv_cols))
            qkv_c = _proj(cs, norm_mix[i], mc[0], mc[1], wqkv, BF16, 512).reshape(b, cl, -1)
            o_x = _attention(qkv_x, qkv_c, attn_sink[j])
            xs = _oproj(o_x, wo, xs, mx[2])
            if not last:
                o_c = _ctx_attention(qkv_c, attn_sink[j]).reshape(1, b * cl, q_cols)
                cs = _oproj(o_c, wo, cs, mc[2])
        else:
            w_in = rec_w_in[j].astype(BF16)
            w_out = rec_w_out[j].astype(BF16)
            w_a = rec_w_a[j].astype(BF16)
            w_x = rec_w_x[j].astype(BF16)
            gu_x = _proj(xs, norm_mix[i], mx[0], mx[1], w_in, F32, 1024)
            gu_c = _proj(cs, norm_mix[i], mc[0], mc[1], w_in, F32, 1024).reshape(b, cl, -1)
            h0 = jnp.zeros((b, 1, d), F32)
            ys_c, ys_x = [], []
            for r in range(2):
                lru = functools.partial(
                    _lru_scan, conv_w=rec_conv_w[j], conv_b=rec_conv_b[j], w_a=w_a[r],
                    b_a=rec_b_a[j, r], w_x=w_x[r], b_x=rec_b_x[j, r], lam=rec_lambda[j, r],
                    reverse=(r == 1))
                y_c, h_c = lru(gu_c, h0=h0)
                y_x, _ = lru(gu_x, h0=h_c)
                ys_c.append(y_c)
                ys_x.append(y_x)
            xs = _rec_oproj(ys_x[0], ys_x[1], gu_x, w_out, xs, mx[2])
            if not last:
                cs = _rec_oproj(ys_c[0].reshape(1, b * cl, d), ys_c[1].reshape(1, b * cl, d),
                                gu_c.reshape(1, b * cl, -1), w_out, cs, mc[2])

        wq = peer_w_q[i].astype(BF16)
        keys = peer_keys[i].astype(BF16)
        u = peer_u[i].astype(BF16)
        vt = peer_v[i].T.astype(BF16)
        xs = _peer(xs, norm_ffn[i], mx[3], mx[4], mx[5], wq, keys, u, vt,
                   final_gain=norm_final if last else None)
        if not last:
            cs = _peer(cs, norm_ffn[i], mc[3], mc[4], mc[5], wq, keys, u, vt)
    return xs
```

```python
import functools
import math

import jax
import jax.numpy as jnp
from jax import lax
from jax.experimental import pallas as pl
from jax.experimental.pallas import tpu as pltpu

F32 = jnp.float32
BF16 = jnp.bfloat16

NORM_EPS = 1e-6
N_MOD = 6
GRID_W = 64
HEAD_DIM = 128
N_Q_HEADS = 16
N_KV_HEADS = 4
GQA_GROUP = N_Q_HEADS // N_KV_HEADS
ATTN_BLOCK = 128
ROPE_BASE = 10000.0
RNN_BLOCKS = 8
CONV_W = 4
CONV_LEFT = 1
LRU_C = 8.0
PEER_HEADS = 8
PEER_KEY_DIM = 128
N_KEYS = 128
PEER_TOPK = 16
PEER_EB = 512

V7X_VMEM_BYTES = 64 * 1024 * 1024
SUBLANES = 8
LANES = 128
VMEM_CAP = V7X_VMEM_BYTES - 6 * 1024 * 1024


def _params(sem, vmem_bytes):
    limit = int(min(VMEM_CAP, max(32 * 1024 * 1024, vmem_bytes * 5 // 4)))
    return pltpu.CompilerParams(dimension_semantics=sem, vmem_limit_bytes=limit)


def _tile(n, pref):
    return pref if n % pref == 0 else n


def _norm_mod(x, gain, shift, scale):
    ms = jnp.mean(x * x, axis=-1, keepdims=True)
    y = x * lax.rsqrt(ms + NORM_EPS)
    return (y * gain) * (1.0 + scale) + shift


def _dot_nt(a, b):
    return lax.dot_general(a, b, (((1,), (1,)), ((), ())), preferred_element_type=F32)


def _mod_body(c_ref, w_ref, b_ref, o_ref):
    s = jax.nn.silu(c_ref[...])
    o_ref[0] = jnp.dot(s.astype(BF16), w_ref[0].astype(BF16), preferred_element_type=F32) + b_ref[0]


def _mod_vectors(cvec, w_mod, b_mod):
    depth, d, n = w_mod.shape
    r = cvec.shape[0]
    tn = _tile(n, 1536)
    vm = 2 * d * tn * 4 + d * tn * 2 + 4 * r * tn * 4
    return pl.pallas_call(
        _mod_body,
        out_shape=jax.ShapeDtypeStruct((depth, r, n), F32),
        grid=(depth, n // tn),
        in_specs=[
            pl.BlockSpec((r, d), lambda i, j: (0, 0)),
            pl.BlockSpec((1, d, tn), lambda i, j: (i, 0, j)),
            pl.BlockSpec((1, 1, tn), lambda i, j: (i, 0, j)),
        ],
        out_specs=pl.BlockSpec((1, r, tn), lambda i, j: (i, 0, j)),
        compiler_params=_params(("arbitrary", "arbitrary"), vm),
        name="mod_vectors",
    )(cvec, w_mod, b_mod.reshape(depth, 1, n))


def _rope(x, cos, sin_signed):
    lane = lax.broadcasted_iota(jnp.int32, x.shape, 1)
    qtr = HEAD_DIM // 4
    first = (lane % (2 * qtr)) < qtr
    rot = jnp.where(first, pltpu.roll(x, HEAD_DIM - qtr, 1), pltpu.roll(x, qtr, 1))
    return x * cos + rot * sin_signed


def _proj_body(*refs, tn, rope_cols):
    if rope_cols:
        x_ref, g_ref, sh_ref, sc_ref, w_ref, cos_ref, sin_ref, o_ref, h_scr = refs
    else:
        x_ref, g_ref, sh_ref, sc_ref, w_ref, o_ref, h_scr = refs
    h_scr[...] = _norm_mod(x_ref[...], g_ref[...], sh_ref[...], sc_ref[...]).astype(BF16)
    for j in range(w_ref.shape[1] // tn):
        ns = slice(j * tn, (j + 1) * tn)
        y = jnp.dot(h_scr[...], w_ref[:, ns], preferred_element_type=F32)
        if j * tn < rope_cols:
            cos = cos_ref[...]
            sin = sin_ref[...]
            y = jnp.concatenate(
                [_rope(y[:, k * HEAD_DIM:(k + 1) * HEAD_DIM], cos, sin)
                 for k in range(tn // HEAD_DIM)], axis=1)
        o_ref[:, ns] = y.astype(o_ref.dtype)


def _proj(x, gain, shift, scale, w, out_dtype, tn, rope=None):
    b, t, d = x.shape
    n = w.shape[1]
    tm = _tile(t, 512)
    in_specs = [
        pl.BlockSpec((None, tm, d), lambda bb, i: (bb, i, 0)),
        pl.BlockSpec((1, d), lambda bb, i: (0, 0)),
        pl.BlockSpec((None, 1, d), lambda bb, i: (bb, 0, 0)),
        pl.BlockSpec((None, 1, d), lambda bb, i: (bb, 0, 0)),
        pl.BlockSpec((d, n), lambda bb, i: (0, 0), pipeline_mode=pl.Buffered(1)),
    ]
    args = [x, gain.reshape(1, d), shift, scale, w]
    rope_cols = 0
    if rope is not None:
        cos, sin_signed, rope_cols = rope
        assert rope_cols % tn == 0
        in_specs += [pl.BlockSpec((tm, HEAD_DIM), lambda bb, i: (i, 0))] * 2
        args += [cos, sin_signed]
    osz = jnp.dtype(out_dtype).itemsize
    vm = 2 * tm * d * 4 + d * n * 2 + 2 * tm * n * osz + tm * d * 2 + 3 * tm * tn * 4
    return pl.pallas_call(
        functools.partial(_proj_body, tn=tn, rope_cols=rope_cols),
        out_shape=jax.ShapeDtypeStruct((b, t, n), out_dtype),
        grid=(b, t // tm),
        in_specs=in_specs,
        out_specs=pl.BlockSpec((None, tm, n), lambda bb, i: (bb, i, 0)),
        scratch_shapes=[pltpu.VMEM((tm, d), BF16)],
        compiler_params=_params(("arbitrary", "arbitrary"), vm),
        name="norm_mod_proj",
    )(*args)


def _oproj_body(a_ref, w_ref, res_ref, gate_ref, o_ref):
    y = jnp.dot(a_ref[...], w_ref[...], preferred_element_type=F32)
    o_ref[...] = res_ref[...] + gate_ref[...] * y


def _rec_oproj_body(yf_ref, yb_ref, gu_ref, w_ref, res_ref, gate_ref, o_ref):
    a = (yf_ref[...] + yb_ref[...]) * jax.nn.gelu(gu_ref[...])
    y = jnp.dot(a.astype(BF16), w_ref[...], preferred_element_type=F32)
    o_ref[...] = res_ref[...] + gate_ref[...] * y


def _oproj(a, w, res, gate):
    b, t, k = a.shape
    d = w.shape[1]
    tm = _tile(t, 512)
    vm = 2 * tm * k * 2 + 2 * k * d * 2 + 4 * tm * d * 4 + tm * d * 4
    return pl.pallas_call(
        _oproj_body,
        out_shape=jax.ShapeDtypeStruct((b, t, d), F32),
        grid=(b, t // tm),
        in_specs=[
            pl.BlockSpec((None, tm, k), lambda bb, i: (bb, i, 0)),
            pl.BlockSpec((k, d), lambda bb, i: (0, 0)),
            pl.BlockSpec((None, tm, d), lambda bb, i: (bb, i, 0)),
            pl.BlockSpec((None, 1, d), lambda bb, i: (bb, 0, 0)),
        ],
        out_specs=pl.BlockSpec((None, tm, d), lambda bb, i: (bb, i, 0)),
        compiler_params=_params(("arbitrary", "arbitrary"), vm),
        name="oproj_residual",
    )(a, w, res, gate)


def _rec_oproj(yf, yb, gu, w, res, gate):
    b, t, k = yf.shape
    d = w.shape[1]
    tm = _tile(t, 256)
    vm = 6 * tm * k * 4 + 2 * k * d * 2 + 4 * tm * d * 4 + 3 * tm * d * 4
    return pl.pallas_call(
        _rec_oproj_body,
        out_shape=jax.ShapeDtypeStruct((b, t, d), F32),
        grid=(b, t // tm),
        in_specs=[
            pl.BlockSpec((None, tm, k), lambda bb, i: (bb, i, 0)),
            pl.BlockSpec((None, tm, k), lambda bb, i: (bb, i, 0)),
            pl.BlockSpec((None, tm, k), lambda bb, i: (bb, i, 0)),
            pl.BlockSpec((k, d), lambda bb, i: (0, 0)),
            pl.BlockSpec((None, tm, d), lambda bb, i: (bb, i, 0)),
            pl.BlockSpec((None, 1, d), lambda bb, i: (bb, 0, 0)),
        ],
        out_specs=pl.BlockSpec((None, tm, d), lambda bb, i: (bb, i, 0)),
        compiler_params=_params(("arbitrary", "arbitrary"), vm),
        name="rec_oproj_residual",
    )(yf, yb, gu, w, res, gate)


def _stack_groups(q):
    return jnp.concatenate(
        [q[:, g * HEAD_DIM:(g + 1) * HEAD_DIM] for g in range(GQA_GROUP)], axis=0)


def _unstack_groups(o, rows):
    return jnp.concatenate([o[g * rows:(g + 1) * rows] for g in range(GQA_GROUP)], axis=1)


def _sink_column(sink_ref, h, rows):
    return jnp.concatenate(
        [jnp.full((rows, 1), sink_ref[h * GQA_GROUP + g], F32) for g in range(GQA_GROUP)], axis=0)


def _attn_body(sink_ref, q_ref, kp_ref, kc_ref, kn_ref, vp_ref, vc_ref, vn_ref, kx_ref, vx_ref,
               o_ref, *, nb):
    i = pl.program_id(1)
    blk = ATTN_BLOCK
    scale = HEAD_DIM ** -0.5
    r = lax.broadcasted_iota(jnp.int32, (GQA_GROUP * blk, blk), 0) % blk
    c = lax.broadcasted_iota(jnp.int32, (GQA_GROUP * blk, blk), 1)
    keep_p = (c >= r) & (i >= 1)
    keep_n = (c <= r) & (i + 1 < nb)
    neg = -jnp.inf
    qw = GQA_GROUP * HEAD_DIM
    for h in range(N_KV_HEADS):
        hs = slice(h * HEAD_DIM, (h + 1) * HEAD_DIM)
        qs = _stack_groups(q_ref[:, h * qw:(h + 1) * qw])
        s_p = jnp.where(keep_p, _dot_nt(qs, kp_ref[:, hs]) * scale, neg)
        s_c = _dot_nt(qs, kc_ref[:, hs]) * scale
        s_n = jnp.where(keep_n, _dot_nt(qs, kn_ref[:, hs]) * scale, neg)
        s_x = _dot_nt(qs, kx_ref[:, hs]) * scale
        sink = _sink_column(sink_ref, h, blk)
        m = jnp.maximum(
            jnp.maximum(jnp.maximum(s_p.max(-1, keepdims=True), s_c.max(-1, keepdims=True)),
                        jnp.maximum(s_n.max(-1, keepdims=True), s_x.max(-1, keepdims=True))),
            sink)
        p_p = jnp.exp(s_p - m)
        p_c = jnp.exp(s_c - m)
        p_n = jnp.exp(s_n - m)
        p_x = jnp.exp(s_x - m)
        denom = (p_p.sum(-1, keepdims=True) + p_c.sum(-1, keepdims=True)
                 + p_n.sum(-1, keepdims=True) + p_x.sum(-1, keepdims=True) + jnp.exp(sink - m))
        o = (jnp.dot(p_p.astype(BF16), vp_ref[:, hs], preferred_element_type=F32)
             + jnp.dot(p_c.astype(BF16), vc_ref[:, hs], preferred_element_type=F32)
             + jnp.dot(p_n.astype(BF16), vn_ref[:, hs], preferred_element_type=F32)
             + jnp.dot(p_x.astype(BF16), vx_ref[:, hs], preferred_element_type=F32))
        o_ref[:, h * qw:(h + 1) * qw] = _unstack_groups(o / denom, blk).astype(o_ref.dtype)


def _attention(qkv_x, qkv_c, sink):
    b, s, _ = qkv_x.shape
    c = qkv_c.shape[1]
    nb = s // ATTN_BLOCK
    q_cols = N_Q_HEADS * HEAD_DIM
    kv_cols = N_KV_HEADS * HEAD_DIM
    k0 = q_cols // kv_cols
    blk = ATTN_BLOCK

    def kv_spec(col, off):
        return pl.BlockSpec((None, blk, kv_cols),
                            lambda bb, i: (bb, jnp.clip(i + off, 0, nb - 1), col))

    in_specs = [
        pl.BlockSpec(memory_space=pltpu.SMEM),
        pl.BlockSpec((None, blk, q_cols), lambda bb, i: (bb, i, 0)),
        kv_spec(k0, -1), kv_spec(k0, 0), kv_spec(k0, 1),
        kv_spec(k0 + 1, -1), kv_spec(k0 + 1, 0), kv_spec(k0 + 1, 1),
        pl.BlockSpec((None, c, kv_cols), lambda bb, i: (bb, 0, k0)),
        pl.BlockSpec((None, c, kv_cols), lambda bb, i: (bb, 0, k0 + 1)),
    ]
    return pl.pallas_call(
        functools.partial(_attn_body, nb=nb),
        out_shape=jax.ShapeDtypeStruct((b, s, q_cols), BF16),
        grid=(b, nb),
        in_specs=in_specs,
        out_specs=pl.BlockSpec((None, blk, q_cols), lambda bb, i: (bb, i, 0)),
        compiler_params=_params(("arbitrary", "arbitrary"), 24 * 1024 * 1024),
        name="window_attention",
    )(sink, qkv_x, qkv_x, qkv_x, qkv_x, qkv_x, qkv_x, qkv_x, qkv_c, qkv_c)


def _ctx_attn_body(sink_ref, q_ref, k_ref, v_ref, o_ref):
    h = pl.program_id(1)
    rows = q_ref.shape[0]
    scale = HEAD_DIM ** -0.5
    qs = _stack_groups(q_ref[...])
    s = _dot_nt(qs, k_ref[...]) * scale
    sink = _sink_column(sink_ref, h, rows)
    m = jnp.maximum(s.max(-1, keepdims=True), sink)
    p = jnp.exp(s - m)
    denom = p.sum(-1, keepdims=True) + jnp.exp(sink - m)
    o = jnp.dot((p / denom).astype(BF16), v_ref[...], preferred_element_type=F32)
    o_ref[...] = _unstack_groups(o, rows).astype(o_ref.dtype)


def _ctx_attention(qkv_c, sink):
    b, c, _ = qkv_c.shape
    qw = GQA_GROUP * HEAD_DIM
    k0 = N_Q_HEADS
    v0 = N_Q_HEADS + N_KV_HEADS
    return pl.pallas_call(
        _ctx_attn_body,
        out_shape=jax.ShapeDtypeStruct((b, c, N_Q_HEADS * HEAD_DIM), BF16),
        grid=(b, N_KV_HEADS),
        in_specs=[
            pl.BlockSpec(memory_space=pltpu.SMEM),
            pl.BlockSpec((None, c, qw), lambda bb, h: (bb, 0, h)),
            pl.BlockSpec((None, c, HEAD_DIM), lambda bb, h: (bb, 0, k0 + h)),
            pl.BlockSpec((None, c, HEAD_DIM), lambda bb, h: (bb, 0, v0 + h)),
        ],
        out_specs=pl.BlockSpec((None, c, qw), lambda bb, h: (bb, 0, h)),
        compiler_params=_params(("arbitrary", "arbitrary"), 16 * 1024 * 1024),
        name="context_attention",
    )(sink, qkv_c, qkv_c, qkv_c)


def _block_diag(ub, w_ref, bias):
    bw = w_ref.shape[1]
    return jnp.concatenate(
        [jnp.dot(ub[:, n * bw:(n + 1) * bw], w_ref[n], preferred_element_type=F32)
         for n in range(w_ref.shape[0])], axis=1) + bias


def _lru_body(up_ref, uc_ref, un_ref, cw_ref, cb_ref, wa_ref, ba_ref, wx_ref, bx_ref, lam_ref,
              h0_ref, y_ref, hl_ref, h_scr, a_scr, b_scr, *, nt, reverse):
    i = pl.program_id(1)
    ti = (nt - 1 - i) if reverse else i
    tm = uc_ref.shape[0]
    halo = up_ref.shape[0]

    @pl.when(i == 0)
    def _():
        h_scr[...] = h0_ref[...]

    prev = jnp.where(ti > 0, up_ref[...], 0.0)
    nxt = jnp.where(ti < nt - 1, un_ref[...], 0.0)
    ext = jnp.concatenate([prev, uc_ref[...], nxt], axis=0)
    u = cb_ref[...]
    for k in range(CONV_W):
        off = halo - CONV_LEFT + k
        u = u + ext[off:off + tm] * cw_ref[k:k + 1, :]

    ub = u.astype(BF16)
    r = jax.nn.sigmoid(_block_diag(ub, wa_ref, ba_ref[...]))
    ig = jax.nn.sigmoid(_block_diag(ub, wx_ref, bx_ref[...]))
    nl = -lam_ref[...]
    softplus = jnp.maximum(nl, 0.0) + jnp.log1p(jnp.exp(-jnp.abs(nl)))
    log_a = -LRU_C * r * softplus
    a_scr[...] = jnp.exp(log_a)
    b_scr[...] = jnp.sqrt(1.0 - jnp.exp(2.0 * log_a)) * ig * u

    ng = tm // SUBLANES
    row = lax.broadcasted_iota(jnp.int32, (SUBLANES, a_scr.shape[1]), 0)

    def group(g, h):
        gi = (ng - 1 - g) if reverse else g
        r0 = pl.multiple_of(gi * SUBLANES, SUBLANES)
        a = a_scr[pl.ds(r0, SUBLANES), :]
        bb = b_scr[pl.ds(r0, SUBLANES), :]
        for k in (1, 2, 4):
            if reverse:
                keep = row < SUBLANES - k
                shift = SUBLANES - k
            else:
                keep = row >= k
                shift = k
            a_sh = pltpu.roll(a, shift, 0)
            b_sh = pltpu.roll(bb, shift, 0)
            bb = bb + a * jnp.where(keep, b_sh, 0.0)
            a = a * jnp.where(keep, a_sh, 1.0)
        y = bb + a * h
        y_ref[pl.ds(r0, SUBLANES), :] = y
        return y[0:1, :] if reverse else y[SUBLANES - 1:SUBLANES, :]

    h_last = lax.fori_loop(0, ng, group, h_scr[...])
    h_scr[...] = h_last

    @pl.when(i == nt - 1)
    def _():
        hl_ref[...] = h_last


def _lru_scan(gu, conv_w, conv_b, w_a, b_a, w_x, b_x, lam, h0, reverse):
    b, t, d2 = gu.shape
    d = d2 // 2
    tm = _tile(t, 256)
    nt = t // tm
    halo = SUBLANES
    hb = tm // halo
    nh = t // halo

    def tmap(i):
        return (nt - 1 - i) if reverse else i

    vec = lambda: pl.BlockSpec((1, d), lambda bb, i: (0, 0))
    in_specs = [
        pl.BlockSpec((None, halo, d), lambda bb, i: (bb, jnp.maximum(tmap(i) * hb - 1, 0), 1)),
        pl.BlockSpec((None, tm, d), lambda bb, i: (bb, tmap(i), 1)),
        pl.BlockSpec((None, halo, d), lambda bb, i: (bb, jnp.minimum((tmap(i) + 1) * hb, nh - 1), 1)),
        pl.BlockSpec((CONV_W, d), lambda bb, i: (0, 0)),
        vec(),
        pl.BlockSpec(w_a.shape, lambda bb, i: (0, 0, 0)),
        vec(),
        pl.BlockSpec(w_x.shape, lambda bb, i: (0, 0, 0)),
        vec(),
        vec(),
        pl.BlockSpec((None, 1, d), lambda bb, i: (bb, 0, 0)),
    ]
    vm = 2 * (tm + 2 * halo) * d * 4 + 2 * tm * d * 4 + 2 * tm * d * 4 + 8 * tm * d * 4 + 4 * w_a.size * 2
    return pl.pallas_call(
        functools.partial(_lru_body, nt=nt, reverse=reverse),
        out_shape=(jax.ShapeDtypeStruct((b, t, d), F32), jax.ShapeDtypeStruct((b, 1, d), F32)),
        grid=(b, nt),
        in_specs=in_specs,
        out_specs=(pl.BlockSpec((None, tm, d), lambda bb, i: (bb, tmap(i), 0)),
                   pl.BlockSpec((None, 1, d), lambda bb, i: (bb, 0, 0))),
        scratch_shapes=[pltpu.VMEM((1, d), F32), pltpu.VMEM((tm, d), F32), pltpu.VMEM((tm, d), F32)],
        compiler_params=_params(("arbitrary", "arbitrary"), vm),
        name="rglru_bwd" if reverse else "rglru_fwd",
    )(gu, gu, gu, conv_w, conv_b.reshape(1, d), w_a, b_a.reshape(1, d), w_x, b_x.reshape(1, d),
      lam.reshape(1, d), h0)


def _peer_query_body(x_ref, g_ref, sh_ref, sc_ref, wq_ref, keys_ref, ht_ref, st_ref):
    h = _norm_mod(x_ref[...], g_ref[...], sh_ref[...], sc_ref[...])
    hb = h.astype(BF16)
    ht_ref[...] = h.T.astype(BF16)
    q = jnp.dot(hb, wq_ref[...], preferred_element_type=F32).astype(BF16)
    for hp in range(2 * PEER_HEADS):
        qc = q[:, hp * PEER_KEY_DIM:(hp + 1) * PEER_KEY_DIM]
        st_ref[hp] = _dot_nt(keys_ref[hp % 2], qc)


def _peer_query(x, gain, shift, scale, wq, keys):
    b, t, d = x.shape
    tm = _tile(t, 512)
    nt = t // tm
    nq = wq.shape[1]
    vm = 2 * tm * d * 4 + 2 * d * nq * 2 + 2 * d * tm * 2 + 2 * 2 * PEER_HEADS * N_KEYS * tm * 4 + 4 * tm * d * 4
    return pl.pallas_call(
        _peer_query_body,
        out_shape=(jax.ShapeDtypeStruct((d, b * t), BF16),
                   jax.ShapeDtypeStruct((2 * PEER_HEADS, N_KEYS, b * t), F32)),
        grid=(b, nt),
        in_specs=[
            pl.BlockSpec((None, tm, d), lambda bb, i: (bb, i, 0)),
            pl.BlockSpec((1, d), lambda bb, i: (0, 0)),
            pl.BlockSpec((None, 1, d), lambda bb, i: (bb, 0, 0)),
            pl.BlockSpec((None, 1, d), lambda bb, i: (bb, 0, 0)),
            pl.BlockSpec((d, nq), lambda bb, i: (0, 0)),
            pl.BlockSpec(keys.shape, lambda bb, i: (0, 0, 0)),
        ],
        out_specs=(pl.BlockSpec((d, tm), lambda bb, i: (0, bb * nt + i)),
                   pl.BlockSpec((2 * PEER_HEADS, N_KEYS, tm), lambda bb, i: (0, 0, bb * nt + i))),
        compiler_params=_params(("arbitrary", "arbitrary"), vm),
        name="peer_query",
    )(x, gain.reshape(1, d), shift, scale, wq, keys)


def _oddeven_merge(lo, hi, r):
    step = r * 2
    if step < hi - lo:
        yield from _oddeven_merge(lo, hi, step)
        yield from _oddeven_merge(lo + r, hi, step)
        yield from [(k, k + r) for k in range(lo + r, hi - r, step)]
    else:
        yield (lo, lo + r)


def _oddeven_sort(lo, hi):
    if hi - lo >= 1:
        mid = lo + (hi - lo) // 2
        yield from _oddeven_sort(lo, mid)
        yield from _oddeven_sort(mid + 1, hi)
        yield from _oddeven_merge(lo, hi, 1)


_SORT16 = tuple(_oddeven_sort(0, PEER_TOPK - 1))


def _exchange(x, p, q):
    hi = jnp.maximum(x[p], x[q])
    lo = jnp.minimum(x[p], x[q])
    x[p] = hi
    x[q] = lo


def _merge_sublanes(x):
    n = len(x)
    shift = SUBLANES // 2
    while shift >= 1:
        z = [jnp.maximum(x[k], pltpu.roll(x[n - 1 - k], shift, 0)) for k in range(n)]
        dist = n // 2
        while dist >= 1:
            for k in range(n):
                if k & dist == 0:
                    _exchange(z, k, k + dist)
            dist //= 2
        x = z
        shift //= 2
    return x


def _top16_sorted(s):
    x = [s[SUBLANES * v:SUBLANES * (v + 1)] for v in range(s.shape[0] // SUBLANES)]
    assert len(x) == PEER_TOPK
    for p, q in _SORT16:
        _exchange(x, p, q)
    return _merge_sublanes(x)


def _dup16(v):
    bits = pltpu.bitcast(v.astype(BF16).astype(F32), jnp.uint32) >> 16
    return bits | (bits << 16)


def _route_stats(st_ref, e1d_scr, c1d_scr, e2_scr, r2_scr):
    tm = st_ref.shape[2]
    row = lax.broadcasted_iota(jnp.int32, (SUBLANES, tm), 0)
    for h in range(PEER_HEADS):
        s1 = st_ref[2 * h]
        s2 = st_ref[2 * h + 1]
        a = _top16_sorted(s1)
        b = _top16_sorted(s2)
        a_lo = a[SUBLANES - 1]
        a_hi = a[2 * SUBLANES - 1]
        for i in range(SUBLANES - 2, -1, -1):
            a_lo = jnp.where(row == i, a[i], a_lo)
            a_hi = jnp.where(row == i, a[SUBLANES + i], a_hi)
        c = [a_lo + b[j] for j in range(PEER_TOPK)]
        d = a_hi + b[0]
        c = [jnp.maximum(c[0], d)] + [
            jnp.maximum(c[j], jnp.minimum(c[j - 1], d)) for j in range(1, PEER_TOPK)]
        t = _merge_sublanes(c)
        z = jnp.ones_like(t[0])
        for k in range(1, PEER_TOPK):
            z = z + jnp.exp(t[k] - t[0])
        inv_z = 1.0 / z
        tau = t[PEER_TOPK - 1][0:1]
        count1 = jnp.zeros_like(s1)
        rank2 = jnp.zeros_like(s2)
        for j in range(PEER_TOPK):
            bj = b[j][0:1]
            count1 = count1 + jnp.where(s1 + bj >= tau, 1.0, 0.0)
            rank2 = rank2 + jnp.where(bj > s2, 1.0, 0.0)
        e1 = jnp.exp(s1 - a[0][0:1]) * inv_z[0:1]
        e2 = jnp.exp(s2 - b[0][0:1])
        for cc in range(tm // LANES):
            cs = slice(cc * LANES, (cc + 1) * LANES)
            c1d_scr[h, cc] = _dup16(count1[:, cs])
            e1d_scr[h, cc] = _dup16(e1[:, cs])
            r2_scr[h, cc] = rank2[:, cs].astype(BF16)
            e2_scr[h, cc] = e2[:, cs].astype(BF16)


def _dense_act(zt_ref, act_ref, blk, cols, e1d_scr, c1d_scr, e2_scr, r2_scr):
    eb = zt_ref.shape[0]
    for l in range(eb // N_KEYS):
        i1 = jnp.clip(blk * (eb // N_KEYS) + l, 0, N_KEYS - 1)
        rs = slice(l * N_KEYS, (l + 1) * N_KEYS)
        for c in cols:
            cs = slice(c * LANES, (c + 1) * LANES)
            g = None
            for h in range(PEER_HEADS):
                cnt = jnp.broadcast_to(c1d_scr[h, c, pl.ds(i1, 1), :], (N_KEYS // 2, LANES))
                e1 = jnp.broadcast_to(e1d_scr[h, c, pl.ds(i1, 1), :], (N_KEYS // 2, LANES))
                gh = jnp.where(r2_scr[h, c] < pltpu.bitcast(cnt, BF16),
                               e2_scr[h, c] * pltpu.bitcast(e1, BF16), jnp.zeros((), BF16))
                g = gh if g is None else g + gh
            act_ref[rs, cs] = jax.nn.gelu(zt_ref[rs, cs]).astype(BF16) * g


def _peer_expert_body(*refs, eb, ns, final_norm):
    if final_norm:
        (ht_ref, st_ref, u_ref, vt_ref, xs_ref, gate_ref, gain_ref, o_ref,
         e1d_scr, c1d_scr, e2_scr, r2_scr, acc_scr, zt_scr, act_scr) = refs
    else:
        (ht_ref, st_ref, u_ref, vt_ref, xs_ref, gate_ref, o_ref,
         e1d_scr, c1d_scr, e2_scr, r2_scr, acc_scr, zt_scr, act_scr) = refs
    j = pl.program_id(2)

    @pl.when(j == 0)
    def _():
        _route_stats(st_ref, e1d_scr, c1d_scr, e2_scr, r2_scr)
        acc_scr[...] = jnp.zeros_like(acc_scr)
        zt_scr[...] = jnp.zeros_like(zt_scr)
        act_scr[...] = jnp.zeros_like(act_scr)

    stats = (e1d_scr, c1d_scr, e2_scr, r2_scr)
    tm = ht_ref.shape[1]
    hw = tm // 2
    halves = [(slice(0, hw), range(0, hw // LANES)),
              (slice(hw, tm), range(hw // LANES, tm // LANES))]
    for s in (0, 1):
        for ts, cols in halves:
            acc_scr[:, ts] += jnp.dot(vt_ref[:, s * eb:(s + 1) * eb], act_scr[s, :, ts],
                                      preferred_element_type=F32)
            _dense_act(zt_scr.at[1 - s], act_scr.at[1 - s], 2 * j - 1 + s, cols, *stats)
            zt_scr[s, :, ts] = jnp.dot(u_ref[s * eb:(s + 1) * eb, :], ht_ref[:, ts],
                                       preferred_element_type=F32)

    @pl.when(j == ns)
    def _():
        y = xs_ref[...] + gate_ref[...] * acc_scr[...].T
        if final_norm:
            ms = jnp.mean(y * y, axis=-1, keepdims=True)
            y = (y * lax.rsqrt(ms + NORM_EPS)) * gain_ref[...]
        o_ref[...] = y


def _peer_experts(ht, st, u, vt, xs, gate, final_gain=None):
    b, t, d = xs.shape
    e = u.shape[0]
    tm = _tile(t, 512)
    nt = t // tm
    nc = tm // LANES
    eb = PEER_EB
    ns = e // (2 * eb)
    assert vt.shape == (ns, d, 2 * eb)
    final_norm = final_gain is not None
    once = pl.Buffered(1)
    in_specs = [
        pl.BlockSpec((d, tm), lambda bb, i, j: (0, bb * nt + i), pipeline_mode=once),
        pl.BlockSpec((2 * PEER_HEADS, N_KEYS, tm), lambda bb, i, j: (0, 0, bb * nt + i),
                     pipeline_mode=once),
        pl.BlockSpec((2 * eb, d), lambda bb, i, j: (jnp.minimum(j, ns - 1), 0)),
        pl.BlockSpec((None, d, 2 * eb), lambda bb, i, j: (jnp.maximum(j - 1, 0), 0, 0)),
        pl.BlockSpec((None, tm, d), lambda bb, i, j: (bb, i, 0), pipeline_mode=once),
        pl.BlockSpec((None, 1, d), lambda bb, i, j: (bb, 0, 0)),
    ]
    args = [ht, st, u, vt, xs, gate]
    if final_norm:
        in_specs.append(pl.BlockSpec((1, d), lambda bb, i, j: (0, 0)))
        args.append(final_gain.reshape(1, d))
    stat = PEER_HEADS * N_KEYS * tm
    vm = (d * tm * 2 + 2 * stat * 4 + 8 * eb * d * 2 + 3 * tm * d * 4
          + stat * (4 + 4 + 2 + 2) + d * tm * 4 + 2 * eb * tm * 6 + 2 * tm * d * 4)
    return pl.pallas_call(
        functools.partial(_peer_expert_body, eb=eb, ns=ns, final_norm=final_norm),
        out_shape=jax.ShapeDtypeStruct((b, t, d), F32),
        grid=(b, nt, ns + 1),
        in_specs=in_specs,
        out_specs=pl.BlockSpec((None, tm, d), lambda bb, i, j: (bb, i, 0)),
        scratch_shapes=[
            pltpu.VMEM((PEER_HEADS, nc, N_KEYS, LANES), jnp.uint32),
            pltpu.VMEM((PEER_HEADS, nc, N_KEYS, LANES), jnp.uint32),
            pltpu.VMEM((PEER_HEADS, nc, N_KEYS, LANES), BF16),
            pltpu.VMEM((PEER_HEADS, nc, N_KEYS, LANES), BF16),
            pltpu.VMEM((d, tm), F32),
            pltpu.VMEM((2, eb, tm), F32),
            pltpu.VMEM((2, eb, tm), BF16),
        ],
        compiler_params=_params(("arbitrary", "arbitrary", "arbitrary"), vm),
        name="peer_experts",
    )(*args)


def _peer(xs, gain, shift, scale, gate, wq, keys, u, vt, final_gain=None):
    ht, st = _peer_query(xs, gain, shift, scale, wq, keys)
    return _peer_experts(ht, st, u, vt, xs, gate, final_gain)


def _rope_tables(s):
    t = jnp.arange(s)
    row = (t // GRID_W).astype(F32)
    col = (t % GRID_W).astype(F32)
    half = HEAD_DIM // 2
    inv = ROPE_BASE ** (-jnp.arange(0, half, 2, dtype=F32) / half)
    ang_r = row[:, None] * inv[None, :]
    ang_c = col[:, None] * inv[None, :]
    ang = jnp.concatenate([ang_r, ang_r, ang_c, ang_c], axis=-1)
    lane = jnp.arange(HEAD_DIM)
    sign = jnp.where((lane % half) < half // 2, -1.0, 1.0).astype(F32)
    return jnp.cos(ang), jnp.sin(ang) * sign[None, :]


def kernel(x, c, ctx, c_ctx, w_mod, b_mod, norm_mix, norm_ffn, norm_final, attn_w_qkv, attn_w_o, attn_sink, rec_w_in, rec_conv_w, rec_conv_b, rec_w_a, rec_b_a, rec_w_x, rec_b_x, rec_lambda, rec_w_out, peer_w_q, peer_keys, peer_u, peer_v):
    b, s, d = x.shape
    cl = ctx.shape[1]
    depth = w_mod.shape[0]
    rows = -(-(b + 1) // SUBLANES) * SUBLANES
    cvec = jnp.zeros((rows, d), F32).at[:b].set(c).at[b].set(c_ctx)
    mod = _mod_vectors(cvec, w_mod, b_mod).reshape(depth, rows, N_MOD, d)
    cos, sin_signed = _rope_tables(s)
    q_cols = N_Q_HEADS * HEAD_DIM
    kv_cols = N_KV_HEADS * HEAD_DIM

    xs = x
    cs = ctx.reshape(1, b * cl, d)
    for i in range(depth):
        last = i == depth - 1
        j = i // 2
        mx = [mod[i, :b, k][:, None, :] for k in range(N_MOD)]
        mc = [mod[i, b:b + 1, k][:, None, :] for k in range(N_MOD)]

        if i % 2 == 0:
            wqkv = attn_w_qkv[j].astype(BF16)
            wo = attn_w_o[j].astype(BF16)
            qkv_x = _proj(xs, norm_mix[i], mx[0], mx[1], wqkv, BF16, 512,
                          rope=(cos, sin_signed, q_cols + kv_cols))
            qkv_c = _proj(cs, norm_mix[i], mc[0], mc[1], wqkv, BF16, 512).reshape(b, cl, -1)
            o_x = _attention(qkv_x, qkv_c, attn_sink[j])
            xs = _oproj(o_x, wo, xs, mx[2])
            if not last:
                o_c = _ctx_attention(qkv_c, attn_sink[j]).reshape(1, b * cl, q_cols)
                cs = _oproj(o_c, wo, cs, mc[2])
        else:
            w_in = rec_w_in[j].astype(BF16)
            w_out = rec_w_out[j].astype(BF16)
            w_a = rec_w_a[j].astype(BF16)
            w_x = rec_w_x[j].astype(BF16)
            gu_x = _proj(xs, norm_mix[i], mx[0], mx[1], w_in, F32, 1024)
            gu_c = _proj(cs, norm_mix[i], mc[0], mc[1], w_in, F32, 1024).reshape(b, cl, -1)
            h0 = jnp.zeros((b, 1, d), F32)
            ys_c, ys_x = [], []
            for r in range(2):
                lru = functools.partial(
                    _lru_scan, conv_w=rec_conv_w[j], conv_b=rec_conv_b[j], w_a=w_a[r],
                    b_a=rec_b_a[j, r], w_x=w_x[r], b_x=rec_b_x[j, r], lam=rec_lambda[j, r],
                    reverse=(r == 1))
                y_c, h_c = lru(gu_c, h0=h0)
                y_x, _ = lru(gu_x, h0=h_c)
                ys_c.append(y_c)
                ys_x.append(y_x)
            xs = _rec_oproj(ys_x[0], ys_x[1], gu_x, w_out, xs, mx[2])
            if not last:
                cs = _rec_oproj(ys_c[0].reshape(1, b * cl, d), ys_c[1].reshape(1, b * cl, d),
                                gu_c.reshape(1, b * cl, -1), w_out, cs, mc[2])

        wq = peer_w_q[i].astype(BF16)
        keys = peer_keys[i].astype(BF16)
        u = peer_u[i].astype(BF16)
        vt = peer_v[i].reshape(-1, 2 * PEER_EB, d).transpose(0, 2, 1).astype(BF16)
        xs = _peer(xs, norm_ffn[i], mx[3], mx[4], mx[5], wq, keys, u, vt,
                   final_gain=norm_final if last else None)
        if not last:
            cs = _peer(cs, norm_ffn[i], mc[3], mc[4], mc[5], wq, keys, u, vt)
    return xs
```

```python
import functools
import math

import jax
import jax.numpy as jnp
from jax import lax
from jax.experimental import pallas as pl
from jax.experimental.pallas import tpu as pltpu

F32 = jnp.float32
BF16 = jnp.bfloat16

NORM_EPS = 1e-6
N_MOD = 6
GRID_W = 64
HEAD_DIM = 128
N_Q_HEADS = 16
N_KV_HEADS = 4
GQA_GROUP = N_Q_HEADS // N_KV_HEADS
ATTN_BLOCK = 128
ROPE_BASE = 10000.0
RNN_BLOCKS = 8
CONV_W = 4
CONV_LEFT = 1
LRU_C = 8.0
PEER_HEADS = 8
PEER_KEY_DIM = 128
N_KEYS = 128
PEER_TOPK = 16
PEER_EB = 512
PEER_SUB = 512
PEER_PIECE = 256

V7X_VMEM_BYTES = 64 * 1024 * 1024
SUBLANES = 8
LANES = 128
VMEM_CAP = V7X_VMEM_BYTES - 6 * 1024 * 1024


def _params(sem, vmem_bytes):
    limit = int(min(VMEM_CAP, max(32 * 1024 * 1024, vmem_bytes * 5 // 4)))
    return pltpu.CompilerParams(dimension_semantics=sem, vmem_limit_bytes=limit)


def _tile(n, pref):
    return pref if n % pref == 0 else n


def _norm_mod(x, gain, shift, scale):
    ms = jnp.mean(x * x, axis=-1, keepdims=True)
    y = x * lax.rsqrt(ms + NORM_EPS)
    return (y * gain) * (1.0 + scale) + shift


def _dot_nt(a, b):
    return lax.dot_general(a, b, (((1,), (1,)), ((), ())), preferred_element_type=F32)


def _mod_body(c_ref, w_ref, b_ref, o_ref):
    s = jax.nn.silu(c_ref[...])
    o_ref[0] = jnp.dot(s.astype(BF16), w_ref[0].astype(BF16), preferred_element_type=F32) + b_ref[0]


def _mod_vectors(cvec, w_mod, b_mod):
    depth, d, n = w_mod.shape
    r = cvec.shape[0]
    tn = _tile(n, 1536)
    vm = 2 * d * tn * 4 + d * tn * 2 + 4 * r * tn * 4
    return pl.pallas_call(
        _mod_body,
        out_shape=jax.ShapeDtypeStruct((depth, r, n), F32),
        grid=(depth, n // tn),
        in_specs=[
            pl.BlockSpec((r, d), lambda i, j: (0, 0)),
            pl.BlockSpec((1, d, tn), lambda i, j: (i, 0, j)),
            pl.BlockSpec((1, 1, tn), lambda i, j: (i, 0, j)),
        ],
        out_specs=pl.BlockSpec((1, r, tn), lambda i, j: (i, 0, j)),
        compiler_params=_params(("arbitrary", "arbitrary"), vm),
        name="mod_vectors",
    )(cvec, w_mod, b_mod.reshape(depth, 1, n))


def _rope(x, cos, sin_signed):
    lane = lax.broadcasted_iota(jnp.int32, x.shape, 1)
    qtr = HEAD_DIM // 4
    first = (lane % (2 * qtr)) < qtr
    rot = jnp.where(first, pltpu.roll(x, HEAD_DIM - qtr, 1), pltpu.roll(x, qtr, 1))
    return x * cos + rot * sin_signed


def _proj_body(*refs, tn, rope_cols):
    if rope_cols:
        x_ref, g_ref, sh_ref, sc_ref, w_ref, cos_ref, sin_ref, o_ref, h_scr = refs
    else:
        x_ref, g_ref, sh_ref, sc_ref, w_ref, o_ref, h_scr = refs
    h_scr[...] = _norm_mod(x_ref[...], g_ref[...], sh_ref[...], sc_ref[...]).astype(BF16)
    for j in range(w_ref.shape[1] // tn):
        ns = slice(j * tn, (j + 1) * tn)
        y = jnp.dot(h_scr[...], w_ref[:, ns], preferred_element_type=F32)
        if j * tn < rope_cols:
            cos = cos_ref[...]
            sin = sin_ref[...]
            y = jnp.concatenate(
                [_rope(y[:, k * HEAD_DIM:(k + 1) * HEAD_DIM], cos, sin)
                 for k in range(tn // HEAD_DIM)], axis=1)
        o_ref[:, ns] = y.astype(o_ref.dtype)


def _proj(x, gain, shift, scale, w, out_dtype, tn, rope=None):
    b, t, d = x.shape
    n = w.shape[1]
    tm = _tile(t, 512)
    in_specs = [
        pl.BlockSpec((None, tm, d), lambda bb, i: (bb, i, 0)),
        pl.BlockSpec((1, d), lambda bb, i: (0, 0)),
        pl.BlockSpec((None, 1, d), lambda bb, i: (bb, 0, 0)),
        pl.BlockSpec((None, 1, d), lambda bb, i: (bb, 0, 0)),
        pl.BlockSpec((d, n), lambda bb, i: (0, 0), pipeline_mode=pl.Buffered(1)),
    ]
    args = [x, gain.reshape(1, d), shift, scale, w]
    rope_cols = 0
    if rope is not None:
        cos, sin_signed, rope_cols = rope
        assert rope_cols % tn == 0
        in_specs += [pl.BlockSpec((tm, HEAD_DIM), lambda bb, i: (i, 0))] * 2
        args += [cos, sin_signed]
    osz = jnp.dtype(out_dtype).itemsize
    vm = 2 * tm * d * 4 + d * n * 2 + 2 * tm * n * osz + tm * d * 2 + 3 * tm * tn * 4
    return pl.pallas_call(
        functools.partial(_proj_body, tn=tn, rope_cols=rope_cols),
        out_shape=jax.ShapeDtypeStruct((b, t, n), out_dtype),
        grid=(b, t // tm),
        in_specs=in_specs,
        out_specs=pl.BlockSpec((None, tm, n), lambda bb, i: (bb, i, 0)),
        scratch_shapes=[pltpu.VMEM((tm, d), BF16)],
        compiler_params=_params(("arbitrary", "arbitrary"), vm),
        name="norm_mod_proj",
    )(*args)


def _oproj_body(a_ref, w_ref, res_ref, gate_ref, o_ref):
    y = jnp.dot(a_ref[...], w_ref[...], preferred_element_type=F32)
    o_ref[...] = res_ref[...] + gate_ref[...] * y


def _rec_oproj_body(yf_ref, yb_ref, gu_ref, w_ref, res_ref, gate_ref, o_ref):
    a = (yf_ref[...] + yb_ref[...]) * jax.nn.gelu(gu_ref[...])
    y = jnp.dot(a.astype(BF16), w_ref[...], preferred_element_type=F32)
    o_ref[...] = res_ref[...] + gate_ref[...] * y


def _oproj(a, w, res, gate):
    b, t, k = a.shape
    d = w.shape[1]
    tm = _tile(t, 512)
    vm = 2 * tm * k * 2 + 2 * k * d * 2 + 4 * tm * d * 4 + tm * d * 4
    return pl.pallas_call(
        _oproj_body,
        out_shape=jax.ShapeDtypeStruct((b, t, d), F32),
        grid=(b, t // tm),
        in_specs=[
            pl.BlockSpec((None, tm, k), lambda bb, i: (bb, i, 0)),
            pl.BlockSpec((k, d), lambda bb, i: (0, 0)),
            pl.BlockSpec((None, tm, d), lambda bb, i: (bb, i, 0)),
            pl.BlockSpec((None, 1, d), lambda bb, i: (bb, 0, 0)),
        ],
        out_specs=pl.BlockSpec((None, tm, d), lambda bb, i: (bb, i, 0)),
        compiler_params=_params(("arbitrary", "arbitrary"), vm),
        name="oproj_residual",
    )(a, w, res, gate)


def _rec_oproj(yf, yb, gu, w, res, gate):
    b, t, k = yf.shape
    d = w.shape[1]
    tm = _tile(t, 256)
    vm = 6 * tm * k * 4 + 2 * k * d * 2 + 4 * tm * d * 4 + 3 * tm * d * 4
    return pl.pallas_call(
        _rec_oproj_body,
        out_shape=jax.ShapeDtypeStruct((b, t, d), F32),
        grid=(b, t // tm),
        in_specs=[
            pl.BlockSpec((None, tm, k), lambda bb, i: (bb, i, 0)),
            pl.BlockSpec((None, tm, k), lambda bb, i: (bb, i, 0)),
            pl.BlockSpec((None, tm, k), lambda bb, i: (bb, i, 0)),
            pl.BlockSpec((k, d), lambda bb, i: (0, 0)),
            pl.BlockSpec((None, tm, d), lambda bb, i: (bb, i, 0)),
            pl.BlockSpec((None, 1, d), lambda bb, i: (bb, 0, 0)),
        ],
        out_specs=pl.BlockSpec((None, tm, d), lambda bb, i: (bb, i, 0)),
        compiler_params=_params(("arbitrary", "arbitrary"), vm),
        name="rec_oproj_residual",
    )(yf, yb, gu, w, res, gate)


def _stack_groups(q):
    return jnp.concatenate(
        [q[:, g * HEAD_DIM:(g + 1) * HEAD_DIM] for g in range(GQA_GROUP)], axis=0)


def _unstack_groups(o, rows):
    return jnp.concatenate([o[g * rows:(g + 1) * rows] for g in range(GQA_GROUP)], axis=1)


def _sink_column(sink_ref, h, rows):
    return jnp.concatenate(
        [jnp.full((rows, 1), sink_ref[h * GQA_GROUP + g], F32) for g in range(GQA_GROUP)], axis=0)


def _attn_body(sink_ref, q_ref, kp_ref, kc_ref, kn_ref, vp_ref, vc_ref, vn_ref, kx_ref, vx_ref,
               o_ref, *, nb):
    i = pl.program_id(1)
    blk = ATTN_BLOCK
    scale = HEAD_DIM ** -0.5
    r = lax.broadcasted_iota(jnp.int32, (GQA_GROUP * blk, blk), 0) % blk
    c = lax.broadcasted_iota(jnp.int32, (GQA_GROUP * blk, blk), 1)
    keep_p = (c >= r) & (i >= 1)
    keep_n = (c <= r) & (i + 1 < nb)
    neg = -jnp.inf
    qw = GQA_GROUP * HEAD_DIM
    for h in range(N_KV_HEADS):
        hs = slice(h * HEAD_DIM, (h + 1) * HEAD_DIM)
        qs = _stack_groups(q_ref[:, h * qw:(h + 1) * qw])
        s_p = jnp.where(keep_p, _dot_nt(qs, kp_ref[:, hs]) * scale, neg)
        s_c = _dot_nt(qs, kc_ref[:, hs]) * scale
        s_n = jnp.where(keep_n, _dot_nt(qs, kn_ref[:, hs]) * scale, neg)
        s_x = _dot_nt(qs, kx_ref[:, hs]) * scale
        sink = _sink_column(sink_ref, h, blk)
        m = jnp.maximum(
            jnp.maximum(jnp.maximum(s_p.max(-1, keepdims=True), s_c.max(-1, keepdims=True)),
                        jnp.maximum(s_n.max(-1, keepdims=True), s_x.max(-1, keepdims=True))),
            sink)
        p_p = jnp.exp(s_p - m)
        p_c = jnp.exp(s_c - m)
        p_n = jnp.exp(s_n - m)
        p_x = jnp.exp(s_x - m)
        denom = (p_p.sum(-1, keepdims=True) + p_c.sum(-1, keepdims=True)
                 + p_n.sum(-1, keepdims=True) + p_x.sum(-1, keepdims=True) + jnp.exp(sink - m))
        o = (jnp.dot(p_p.astype(BF16), vp_ref[:, hs], preferred_element_type=F32)
             + jnp.dot(p_c.astype(BF16), vc_ref[:, hs], preferred_element_type=F32)
             + jnp.dot(p_n.astype(BF16), vn_ref[:, hs], preferred_element_type=F32)
             + jnp.dot(p_x.astype(BF16), vx_ref[:, hs], preferred_element_type=F32))
        o_ref[:, h * qw:(h + 1) * qw] = _unstack_groups(o / denom, blk).astype(o_ref.dtype)


def _attention(qkv_x, qkv_c, sink):
    b, s, _ = qkv_x.shape
    c = qkv_c.shape[1]
    nb = s // ATTN_BLOCK
    q_cols = N_Q_HEADS * HEAD_DIM
    kv_cols = N_KV_HEADS * HEAD_DIM
    k0 = q_cols // kv_cols
    blk = ATTN_BLOCK

    def kv_spec(col, off):
        return pl.BlockSpec((None, blk, kv_cols),
                            lambda bb, i: (bb, jnp.clip(i + off, 0, nb - 1), col))

    in_specs = [
        pl.BlockSpec(memory_space=pltpu.SMEM),
        pl.BlockSpec((None, blk, q_cols), lambda bb, i: (bb, i, 0)),
        kv_spec(k0, -1), kv_spec(k0, 0), kv_spec(k0, 1),
        kv_spec(k0 + 1, -1), kv_spec(k0 + 1, 0), kv_spec(k0 + 1, 1),
        pl.BlockSpec((None, c, kv_cols), lambda bb, i: (bb, 0, k0)),
        pl.BlockSpec((None, c, kv_cols), lambda bb, i: (bb, 0, k0 + 1)),
    ]
    return pl.pallas_call(
        functools.partial(_attn_body, nb=nb),
        out_shape=jax.ShapeDtypeStruct((b, s, q_cols), BF16),
        grid=(b, nb),
        in_specs=in_specs,
        out_specs=pl.BlockSpec((None, blk, q_cols), lambda bb, i: (bb, i, 0)),
        compiler_params=_params(("arbitrary", "arbitrary"), 24 * 1024 * 1024),
        name="window_attention",
    )(sink, qkv_x, qkv_x, qkv_x, qkv_x, qkv_x, qkv_x, qkv_x, qkv_c, qkv_c)


def _ctx_attn_body(sink_ref, q_ref, k_ref, v_ref, o_ref):
    h = pl.program_id(1)
    rows = q_ref.shape[0]
    scale = HEAD_DIM ** -0.5
    qs = _stack_groups(q_ref[...])
    s = _dot_nt(qs, k_ref[...]) * scale
    sink = _sink_column(sink_ref, h, rows)
    m = jnp.maximum(s.max(-1, keepdims=True), sink)
    p = jnp.exp(s - m)
    denom = p.sum(-1, keepdims=True) + jnp.exp(sink - m)
    o = jnp.dot((p / denom).astype(BF16), v_ref[...], preferred_element_type=F32)
    o_ref[...] = _unstack_groups(o, rows).astype(o_ref.dtype)


def _ctx_attention(qkv_c, sink):
    b, c, _ = qkv_c.shape
    qw = GQA_GROUP * HEAD_DIM
    k0 = N_Q_HEADS
    v0 = N_Q_HEADS + N_KV_HEADS
    return pl.pallas_call(
        _ctx_attn_body,
        out_shape=jax.ShapeDtypeStruct((b, c, N_Q_HEADS * HEAD_DIM), BF16),
        grid=(b, N_KV_HEADS),
        in_specs=[
            pl.BlockSpec(memory_space=pltpu.SMEM),
            pl.BlockSpec((None, c, qw), lambda bb, h: (bb, 0, h)),
            pl.BlockSpec((None, c, HEAD_DIM), lambda bb, h: (bb, 0, k0 + h)),
            pl.BlockSpec((None, c, HEAD_DIM), lambda bb, h: (bb, 0, v0 + h)),
        ],
        out_specs=pl.BlockSpec((None, c, qw), lambda bb, h: (bb, 0, h)),
        compiler_params=_params(("arbitrary", "arbitrary"), 16 * 1024 * 1024),
        name="context_attention",
    )(sink, qkv_c, qkv_c, qkv_c)


def _block_diag(ub, w_ref, bias):
    bw = w_ref.shape[1]
    return jnp.concatenate(
        [jnp.dot(ub[:, n * bw:(n + 1) * bw], w_ref[n], preferred_element_type=F32)
         for n in range(w_ref.shape[0])], axis=1) + bias


def _lru_body(up_ref, uc_ref, un_ref, cw_ref, cb_ref, wa_ref, ba_ref, wx_ref, bx_ref, lam_ref,
              h0_ref, y_ref, hl_ref, h_scr, a_scr, b_scr, *, nt, reverse):
    i = pl.program_id(1)
    ti = (nt - 1 - i) if reverse else i
    tm = uc_ref.shape[0]
    halo = up_ref.shape[0]

    @pl.when(i == 0)
    def _():
        h_scr[...] = h0_ref[...]

    prev = jnp.where(ti > 0, up_ref[...], 0.0)
    nxt = jnp.where(ti < nt - 1, un_ref[...], 0.0)
    ext = jnp.concatenate([prev, uc_ref[...], nxt], axis=0)
    u = cb_ref[...]
    for k in range(CONV_W):
        off = halo - CONV_LEFT + k
        u = u + ext[off:off + tm] * cw_ref[k:k + 1, :]

    ub = u.astype(BF16)
    r = jax.nn.sigmoid(_block_diag(ub, wa_ref, ba_ref[...]))
    ig = jax.nn.sigmoid(_block_diag(ub, wx_ref, bx_ref[...]))
    nl = -lam_ref[...]
    softplus = jnp.maximum(nl, 0.0) + jnp.log1p(jnp.exp(-jnp.abs(nl)))
    log_a = -LRU_C * r * softplus
    a_scr[...] = jnp.exp(log_a)
    b_scr[...] = jnp.sqrt(1.0 - jnp.exp(2.0 * log_a)) * ig * u

    ng = tm // SUBLANES
    row = lax.broadcasted_iota(jnp.int32, (SUBLANES, a_scr.shape[1]), 0)

    def group(g, h):
        gi = (ng - 1 - g) if reverse else g
        r0 = pl.multiple_of(gi * SUBLANES, SUBLANES)
        a = a_scr[pl.ds(r0, SUBLANES), :]
        bb = b_scr[pl.ds(r0, SUBLANES), :]
        for k in (1, 2, 4):
            if reverse:
                keep = row < SUBLANES - k
                shift = SUBLANES - k
            else:
                keep = row >= k
                shift = k
            a_sh = pltpu.roll(a, shift, 0)
            b_sh = pltpu.roll(bb, shift, 0)
            bb = bb + a * jnp.where(keep, b_sh, 0.0)
            a = a * jnp.where(keep, a_sh, 1.0)
        y = bb + a * h
        y_ref[pl.ds(r0, SUBLANES), :] = y
        return y[0:1, :] if reverse else y[SUBLANES - 1:SUBLANES, :]

    h_last = lax.fori_loop(0, ng, group, h_scr[...])
    h_scr[...] = h_last

    @pl.when(i == nt - 1)
    def _():
        hl_ref[...] = h_last


def _lru_scan(gu, conv_w, conv_b, w_a, b_a, w_x, b_x, lam, h0, reverse):
    b, t, d2 = gu.shape
    d = d2 // 2
    tm = _tile(t, 256)
    nt = t // tm
    halo = SUBLANES
    hb = tm // halo
    nh = t // halo

    def tmap(i):
        return (nt - 1 - i) if reverse else i

    vec = lambda: pl.BlockSpec((1, d), lambda bb, i: (0, 0))
    in_specs = [
        pl.BlockSpec((None, halo, d), lambda bb, i: (bb, jnp.maximum(tmap(i) * hb - 1, 0), 1)),
        pl.BlockSpec((None, tm, d), lambda bb, i: (bb, tmap(i), 1)),
        pl.BlockSpec((None, halo, d), lambda bb, i: (bb, jnp.minimum((tmap(i) + 1) * hb, nh - 1), 1)),
        pl.BlockSpec((CONV_W, d), lambda bb, i: (0, 0)),
        vec(),
        pl.BlockSpec(w_a.shape, lambda bb, i: (0, 0, 0)),
        vec(),
        pl.BlockSpec(w_x.shape, lambda bb, i: (0, 0, 0)),
        vec(),
        vec(),
        pl.BlockSpec((None, 1, d), lambda bb, i: (bb, 0, 0)),
    ]
    vm = 2 * (tm + 2 * halo) * d * 4 + 2 * tm * d * 4 + 2 * tm * d * 4 + 8 * tm * d * 4 + 4 * w_a.size * 2
    return pl.pallas_call(
        functools.partial(_lru_body, nt=nt, reverse=reverse),
        out_shape=(jax.ShapeDtypeStruct((b, t, d), F32), jax.ShapeDtypeStruct((b, 1, d), F32)),
        grid=(b, nt),
        in_specs=in_specs,
        out_specs=(pl.BlockSpec((None, tm, d), lambda bb, i: (bb, tmap(i), 0)),
                   pl.BlockSpec((None, 1, d), lambda bb, i: (bb, 0, 0))),
        scratch_shapes=[pltpu.VMEM((1, d), F32), pltpu.VMEM((tm, d), F32), pltpu.VMEM((tm, d), F32)],
        compiler_params=_params(("arbitrary", "arbitrary"), vm),
        name="rglru_bwd" if reverse else "rglru_fwd",
    )(gu, gu, gu, conv_w, conv_b.reshape(1, d), w_a, b_a.reshape(1, d), w_x, b_x.reshape(1, d),
      lam.reshape(1, d), h0)


def _peer_query_body(x_ref, g_ref, sh_ref, sc_ref, wq_ref, keys_ref, ht_ref, st_ref):
    h = _norm_mod(x_ref[...], g_ref[...], sh_ref[...], sc_ref[...])
    hb = h.astype(BF16)
    ht = h.T.astype(BF16)
    for p in range(ht_ref.shape[0]):
        ht_ref[p] = ht[:, p * PEER_PIECE:(p + 1) * PEER_PIECE]
    q = jnp.dot(hb, wq_ref[...], preferred_element_type=F32).astype(BF16)
    for hp in range(2 * PEER_HEADS):
        qc = q[:, hp * PEER_KEY_DIM:(hp + 1) * PEER_KEY_DIM]
        st_ref[hp] = _dot_nt(keys_ref[hp % 2], qc)


def _peer_query(x, gain, shift, scale, wq, keys):
    b, t, d = x.shape
    tm = _tile(t, PEER_SUB)
    nt = t // tm
    nq = wq.shape[1]
    vm = 2 * tm * d * 4 + 2 * d * nq * 2 + 2 * d * tm * 2 + 2 * 2 * PEER_HEADS * N_KEYS * tm * 4 + 4 * tm * d * 4
    return pl.pallas_call(
        _peer_query_body,
        out_shape=(jax.ShapeDtypeStruct((b * nt, tm // PEER_PIECE, d, PEER_PIECE), BF16),
                   jax.ShapeDtypeStruct((2 * PEER_HEADS, N_KEYS, b * t), F32)),
        grid=(b, nt),
        in_specs=[
            pl.BlockSpec((None, tm, d), lambda bb, i: (bb, i, 0)),
            pl.BlockSpec((1, d), lambda bb, i: (0, 0)),
            pl.BlockSpec((None, 1, d), lambda bb, i: (bb, 0, 0)),
            pl.BlockSpec((None, 1, d), lambda bb, i: (bb, 0, 0)),
            pl.BlockSpec((d, nq), lambda bb, i: (0, 0)),
            pl.BlockSpec(keys.shape, lambda bb, i: (0, 0, 0)),
        ],
        out_specs=(pl.BlockSpec((None, tm // PEER_PIECE, d, PEER_PIECE),
                                lambda bb, i: (bb * nt + i, 0, 0, 0)),
                   pl.BlockSpec((2 * PEER_HEADS, N_KEYS, tm), lambda bb, i: (0, 0, bb * nt + i))),
        compiler_params=_params(("arbitrary", "arbitrary"), vm),
        name="peer_query",
    )(x, gain.reshape(1, d), shift, scale, wq, keys)


def _oddeven_merge(lo, hi, r):
    step = r * 2
    if step < hi - lo:
        yield from _oddeven_merge(lo, hi, step)
        yield from _oddeven_merge(lo + r, hi, step)
        yield from [(k, k + r) for k in range(lo + r, hi - r, step)]
    else:
        yield (lo, lo + r)


def _oddeven_sort(lo, hi):
    if hi - lo >= 1:
        mid = lo + (hi - lo) // 2
        yield from _oddeven_sort(lo, mid)
        yield from _oddeven_sort(mid + 1, hi)
        yield from _oddeven_merge(lo, hi, 1)


_SORT16 = tuple(_oddeven_sort(0, PEER_TOPK - 1))


def _exchange(x, p, q):
    hi = jnp.maximum(x[p], x[q])
    lo = jnp.minimum(x[p], x[q])
    x[p] = hi
    x[q] = lo


def _merge_sublanes(x):
    n = len(x)
    shift = SUBLANES // 2
    while shift >= 1:
        z = [jnp.maximum(x[k], pltpu.roll(x[n - 1 - k], shift, 0)) for k in range(n)]
        dist = n // 2
        while dist >= 1:
            for k in range(n):
                if k & dist == 0:
                    _exchange(z, k, k + dist)
            dist //= 2
        x = z
        shift //= 2
    return x


def _top16_sorted(s):
    x = [s[SUBLANES * v:SUBLANES * (v + 1)] for v in range(s.shape[0] // SUBLANES)]
    assert len(x) == PEER_TOPK
    for p, q in _SORT16:
        _exchange(x, p, q)
    return _merge_sublanes(x)


def _dup16(v):
    bits = pltpu.bitcast(v.astype(BF16).astype(F32), jnp.uint32) >> 16
    return bits | (bits << 16)


def _route_stats(st_ref, e1d_scr, c1d_scr, e2_scr, r2_scr):
    tm = st_ref.shape[2]
    row = lax.broadcasted_iota(jnp.int32, (SUBLANES, tm), 0)
    for h in range(PEER_HEADS):
        s1 = st_ref[2 * h]
        s2 = st_ref[2 * h + 1]
        a = _top16_sorted(s1)
        b = _top16_sorted(s2)
        a_lo = a[SUBLANES - 1]
        a_hi = a[2 * SUBLANES - 1]
        for i in range(SUBLANES - 2, -1, -1):
            a_lo = jnp.where(row == i, a[i], a_lo)
            a_hi = jnp.where(row == i, a[SUBLANES + i], a_hi)
        c = [a_lo + b[j] for j in range(PEER_TOPK)]
        d = a_hi + b[0]
        c = [jnp.maximum(c[0], d)] + [
            jnp.maximum(c[j], jnp.minimum(c[j - 1], d)) for j in range(1, PEER_TOPK)]
        t = _merge_sublanes(c)
        z = jnp.ones_like(t[0])
        for k in range(1, PEER_TOPK):
            z = z + jnp.exp(t[k] - t[0])
        inv_z = 1.0 / z
        tau = t[PEER_TOPK - 1][0:1]
        count1 = jnp.zeros_like(s1)
        rank2 = jnp.zeros_like(s2)
        for j in range(PEER_TOPK):
            bj = b[j][0:1]
            count1 = count1 + jnp.where(s1 + bj >= tau, 1.0, 0.0)
            rank2 = rank2 + jnp.where(bj > s2, 1.0, 0.0)
        e1 = jnp.exp(s1 - a[0][0:1]) * inv_z[0:1]
        e2 = jnp.exp(s2 - b[0][0:1])
        for cc in range(tm // LANES):
            cs = slice(cc * LANES, (cc + 1) * LANES)
            c1d_scr[h, cc] = _dup16(count1[:, cs])
            e1d_scr[h, cc] = _dup16(e1[:, cs])
            r2_scr[h, cc] = rank2[:, cs].astype(BF16)
            e2_scr[h, cc] = e2[:, cs].astype(BF16)


def _route_body(st_ref, e1d_ref, c1d_ref, e2_ref, r2_ref):
    _route_stats(st_ref, e1d_ref, c1d_ref, e2_ref, r2_ref)


def _peer_route(st):
    t = st.shape[2]
    tm = _tile(t, PEER_SUB)
    nc = tm // LANES
    shape = (PEER_HEADS, t // LANES, N_KEYS, LANES)
    spec = pl.BlockSpec((PEER_HEADS, nc, N_KEYS, LANES), lambda i: (0, i, 0, 0))
    stat = PEER_HEADS * N_KEYS * tm
    return pl.pallas_call(
        _route_body,
        out_shape=(jax.ShapeDtypeStruct(shape, jnp.uint32), jax.ShapeDtypeStruct(shape, jnp.uint32),
                   jax.ShapeDtypeStruct(shape, BF16), jax.ShapeDtypeStruct(shape, BF16)),
        grid=(t // tm,),
        in_specs=[pl.BlockSpec((2 * PEER_HEADS, N_KEYS, tm), lambda i: (0, 0, i))],
        out_specs=(spec, spec, spec, spec),
        compiler_params=_params(("arbitrary",), 2 * 2 * stat * 4 + 2 * stat * 12 + 8 * stat),
        name="peer_route",
    )(st)


def _dense_act(zt_ref, act_ref, blk, c0, cols, e1d_ref, c1d_ref, e2_ref, r2_ref):
    eb = zt_ref.shape[1]
    for l in range(eb // N_KEYS):
        i1 = jnp.clip(blk * (eb // N_KEYS) + l, 0, N_KEYS - 1)
        rs = slice(l * N_KEYS, (l + 1) * N_KEYS)
        for c in cols:
            g = None
            for h in range(PEER_HEADS):
                cnt = jnp.broadcast_to(c1d_ref[h, c0 + c, pl.ds(i1, 1), :], (N_KEYS // 2, LANES))
                e1 = jnp.broadcast_to(e1d_ref[h, c0 + c, pl.ds(i1, 1), :], (N_KEYS // 2, LANES))
                gh = jnp.where(r2_ref[h, c0 + c] < pltpu.bitcast(cnt, BF16),
                               e2_ref[h, c0 + c] * pltpu.bitcast(e1, BF16), jnp.zeros((), BF16))
                g = gh if g is None else g + gh
            act_ref[c, rs, :] = jax.nn.gelu(zt_ref[c, rs, :]).astype(BF16) * g


def _peer_expert_body(ht_ref, e1d_hbm, c1d_hbm, e2_hbm, r2_hbm, u_ref, vt_ref, o_ref,
                      e1d_scr, c1d_scr, e2_scr, r2_scr, gate_sem, zt_scr, act_scr, *, eb):
    j = pl.program_id(1)
    gates = (e1d_scr, c1d_scr, e2_scr, r2_scr)

    @pl.when(j == 0)
    def _():
        gnc = e1d_scr.shape[1]
        first = pl.multiple_of(pl.program_id(0) * gnc, gnc)
        copies = [
            pltpu.make_async_copy(src.at[:, pl.ds(first, gnc)], dst, gate_sem.at[n])
            for n, (src, dst) in enumerate(zip((e1d_hbm, c1d_hbm, e2_hbm, r2_hbm), gates))]
        for cp in copies:
            cp.start()
        o_ref[...] = jnp.zeros_like(o_ref)
        zt_scr[...] = jnp.zeros_like(zt_scr)
        act_scr[...] = jnp.zeros_like(act_scr)
        for cp in copies:
            cp.wait()

    n_sub, n_pc, _, pw = ht_ref.shape
    cpp = pw // LANES
    nc = n_pc * cpp

    def sub_tile(k, carry):
        for s in (0, 1):
            for p in range(n_pc):
                cols = range(p * cpp, (p + 1) * cpp)
                act = jnp.concatenate([act_scr[k, s, c] for c in cols], axis=1)
                o_ref[k, p] += jnp.dot(vt_ref[s], act, preferred_element_type=F32)
                _dense_act(zt_scr.at[k, 1 - s], act_scr.at[k, 1 - s], 2 * j - 1 + s, k * nc, cols,
                           *gates)
                z = jnp.dot(u_ref[s * eb:(s + 1) * eb, :], ht_ref[k, p],
                            preferred_element_type=F32)
                for n, c in enumerate(cols):
                    zt_scr[k, s, c] = z[:, n * LANES:(n + 1) * LANES]
        return carry

    lax.fori_loop(0, n_sub, sub_tile, 0)


def _peer_experts(ht, gates, u, vt):
    n_tiles, n_pc, d, pw = ht.shape
    tm = n_pc * pw
    e = u.shape[0]
    g = 2 if n_tiles % 2 == 0 else 1
    nc = tm // LANES
    eb = PEER_EB
    ns = e // (2 * eb)
    assert vt.shape == (ns, 2, d, eb)
    once = pl.Buffered(1)
    gate_spec = pl.BlockSpec(memory_space=pl.ANY)
    gate_shape = (PEER_HEADS, g * nc, N_KEYS, LANES)
    in_specs = [
        pl.BlockSpec((g, n_pc, d, pw), lambda i, j: (i, 0, 0, 0), pipeline_mode=once),
        gate_spec, gate_spec, gate_spec, gate_spec,
        pl.BlockSpec((2 * eb, d), lambda i, j: (jnp.minimum(j, ns - 1), 0)),
        pl.BlockSpec((None, 2, d, eb), lambda i, j: (jnp.maximum(j - 1, 0), 0, 0, 0)),
    ]
    stat = PEER_HEADS * N_KEYS * tm
    vm = g * (d * tm * 2 + stat * 12 + 2 * d * tm * 4 + 2 * eb * tm * 6) + 8 * eb * d * 2
    return pl.pallas_call(
        functools.partial(_peer_expert_body, eb=eb),
        out_shape=jax.ShapeDtypeStruct((n_tiles, n_pc, d, pw), F32),
        grid=(n_tiles // g, ns + 1),
        in_specs=in_specs,
        out_specs=pl.BlockSpec((g, n_pc, d, pw), lambda i, j: (i, 0, 0, 0)),
        scratch_shapes=[
            pltpu.VMEM(gate_shape, jnp.uint32),
            pltpu.VMEM(gate_shape, jnp.uint32),
            pltpu.VMEM(gate_shape, BF16),
            pltpu.VMEM(gate_shape, BF16),
            pltpu.SemaphoreType.DMA((4,)),
            pltpu.VMEM((g, 2, nc, eb, LANES), F32),
            pltpu.VMEM((g, 2, nc, eb, LANES), BF16),
        ],
        compiler_params=_params(("arbitrary", "arbitrary"), vm),
        name="peer_experts",
    )(ht, *gates, u, vt)


def _peer_residual_body(*refs, final_norm):
    if final_norm:
        xs_ref, gate_ref, ft_ref, gain_ref, o_ref = refs
    else:
        xs_ref, gate_ref, ft_ref, o_ref = refs
    f = jnp.concatenate([ft_ref[p].T for p in range(ft_ref.shape[0])], axis=0)
    y = xs_ref[...] + gate_ref[...] * f
    if final_norm:
        ms = jnp.mean(y * y, axis=-1, keepdims=True)
        y = (y * lax.rsqrt(ms + NORM_EPS)) * gain_ref[...]
    o_ref[...] = y


def _peer_residual(xs, gate, ft, final_gain=None):
    b, t, d = xs.shape
    n_pc, pw = ft.shape[1], ft.shape[3]
    tm = n_pc * pw
    nt = t // tm
    final_norm = final_gain is not None
    in_specs = [
        pl.BlockSpec((None, tm, d), lambda bb, i: (bb, i, 0)),
        pl.BlockSpec((None, 1, d), lambda bb, i: (bb, 0, 0)),
        pl.BlockSpec((None, n_pc, d, pw), lambda bb, i: (bb * nt + i, 0, 0, 0)),
    ]
    args = [xs, gate, ft]
    if final_norm:
        in_specs.append(pl.BlockSpec((1, d), lambda bb, i: (0, 0)))
        args.append(final_gain.reshape(1, d))
    return pl.pallas_call(
        functools.partial(_peer_residual_body, final_norm=final_norm),
        out_shape=jax.ShapeDtypeStruct((b, t, d), F32),
        grid=(b, nt),
        in_specs=in_specs,
        out_specs=pl.BlockSpec((None, tm, d), lambda bb, i: (bb, i, 0)),
        compiler_params=_params(("arbitrary", "arbitrary"), 8 * tm * d * 4),
        name="peer_residual",
    )(*args)


def _peer(xs, gain, shift, scale, gate, wq, keys, u, vt, final_gain=None):
    ht, st = _peer_query(xs, gain, shift, scale, wq, keys)
    ft = _peer_experts(ht, _peer_route(st), u, vt)
    return _peer_residual(xs, gate, ft, final_gain)


def _rope_tables(s):
    t = jnp.arange(s)
    row = (t // GRID_W).astype(F32)
    col = (t % GRID_W).astype(F32)
    half = HEAD_DIM // 2
    inv = ROPE_BASE ** (-jnp.arange(0, half, 2, dtype=F32) / half)
    ang_r = row[:, None] * inv[None, :]
    ang_c = col[:, None] * inv[None, :]
    ang = jnp.concatenate([ang_r, ang_r, ang_c, ang_c], axis=-1)
    lane = jnp.arange(HEAD_DIM)
    sign = jnp.where((lane % half) < half // 2, -1.0, 1.0).astype(F32)
    return jnp.cos(ang), jnp.sin(ang) * sign[None, :]


def kernel(x, c, ctx, c_ctx, w_mod, b_mod, norm_mix, norm_ffn, norm_final, attn_w_qkv, attn_w_o, attn_sink, rec_w_in, rec_conv_w, rec_conv_b, rec_w_a, rec_b_a, rec_w_x, rec_b_x, rec_lambda, rec_w_out, peer_w_q, peer_keys, peer_u, peer_v):
    b, s, d = x.shape
    cl = ctx.shape[1]
    depth = w_mod.shape[0]
    rows = -(-(b + 1) // SUBLANES) * SUBLANES
    cvec = jnp.zeros((rows, d), F32).at[:b].set(c).at[b].set(c_ctx)
    mod = _mod_vectors(cvec, w_mod, b_mod).reshape(depth, rows, N_MOD, d)
    cos, sin_signed = _rope_tables(s)
    q_cols = N_Q_HEADS * HEAD_DIM
    kv_cols = N_KV_HEADS * HEAD_DIM

    xs = x
    cs = ctx.reshape(1, b * cl, d)
    for i in range(depth):
        last = i == depth - 1
        j = i // 2
        mx = [mod[i, :b, k][:, None, :] for k in range(N_MOD)]
        mc = [mod[i, b:b + 1, k][:, None, :] for k in range(N_MOD)]

        if i % 2 == 0:
            wqkv = attn_w_qkv[j].astype(BF16)
            wo = attn_w_o[j].astype(BF16)
            qkv_x = _proj(xs, norm_mix[i], mx[0], mx[1], wqkv, BF16, 512,
                          rope=(cos, sin_signed, q_cols + kv_cols))
            qkv_c = _proj(cs, norm_mix[i], mc[0], mc[1], wqkv, BF16, 512).reshape(b, cl, -1)
            o_x = _attention(qkv_x, qkv_c, attn_sink[j])
            xs = _oproj(o_x, wo, xs, mx[2])
            if not last:
                o_c = _ctx_attention(qkv_c, attn_sink[j]).reshape(1, b * cl, q_cols)
                cs = _oproj(o_c, wo, cs, mc[2])
        else:
            w_in = rec_w_in[j].astype(BF16)
            w_out = rec_w_out[j].astype(BF16)
            w_a = rec_w_a[j].astype(BF16)
            w_x = rec_w_x[j].astype(BF16)
            gu_x = _proj(xs, norm_mix[i], mx[0], mx[1], w_in, F32, 1024)
            gu_c = _proj(cs, norm_mix[i], mc[0], mc[1], w_in, F32, 1024).reshape(b, cl, -1)
            h0 = jnp.zeros((b, 1, d), F32)
            ys_c, ys_x = [], []
            for r in range(2):
                lru = functools.partial(
                    _lru_scan, conv_w=rec_conv_w[j], conv_b=rec_conv_b[j], w_a=w_a[r],
                    b_a=rec_b_a[j, r], w_x=w_x[r], b_x=rec_b_x[j, r], lam=rec_lambda[j, r],
                    reverse=(r == 1))
                y_c, h_c = lru(gu_c, h0=h0)
                y_x, _ = lru(gu_x, h0=h_c)
                ys_c.append(y_c)
                ys_x.append(y_x)
            xs = _rec_oproj(ys_x[0], ys_x[1], gu_x, w_out, xs, mx[2])
            if not last:
                cs = _rec_oproj(ys_c[0].reshape(1, b * cl, d), ys_c[1].reshape(1, b * cl, d),
                                gu_c.reshape(1, b * cl, -1), w_out, cs, mc[2])

        wq = peer_w_q[i].astype(BF16)
        keys = peer_keys[i].astype(BF16)
        u = peer_u[i].astype(BF16)
        vt = peer_v[i].reshape(-1, 2, PEER_EB, d).transpose(0, 1, 3, 2).astype(BF16)
        xs = _peer(xs, norm_ffn[i], mx[3], mx[4], mx[5], wq, keys, u, vt,
                   final_gain=norm_final if last else None)
        if not last:
            cs = _peer(cs, norm_ffn[i], mc[3], mc[4], mc[5], wq, keys, u, vt)
    return xs
```

```python
import functools
import math

import jax
import jax.numpy as jnp
from jax import lax
from jax.experimental import pallas as pl
from jax.experimental.pallas import tpu as pltpu

F32 = jnp.float32
BF16 = jnp.bfloat16

NORM_EPS = 1e-6
N_MOD = 6
GRID_W = 64
HEAD_DIM = 128
N_Q_HEADS = 16
N_KV_HEADS = 4
GQA_GROUP = N_Q_HEADS // N_KV_HEADS
ATTN_BLOCK = 128
ROPE_BASE = 10000.0
RNN_BLOCKS = 8
CONV_W = 4
CONV_LEFT = 1
LRU_C = 8.0
PEER_HEADS = 8
PEER_KEY_DIM = 128
N_KEYS = 128
PEER_TOPK = 16
PEER_EB = 512
PEER_SUB = 512
PEER_PIECE = 256

V7X_VMEM_BYTES = 64 * 1024 * 1024
SUBLANES = 8
LANES = 128
VMEM_CAP = V7X_VMEM_BYTES - 6 * 1024 * 1024


def _params(sem, vmem_bytes):
    limit = int(min(VMEM_CAP, max(32 * 1024 * 1024, vmem_bytes * 5 // 4)))
    return pltpu.CompilerParams(dimension_semantics=sem, vmem_limit_bytes=limit)


def _tile(n, pref):
    return pref if n % pref == 0 else n


def _norm_mod(x, gain, shift, scale):
    ms = jnp.mean(x * x, axis=-1, keepdims=True)
    y = x * lax.rsqrt(ms + NORM_EPS)
    return (y * gain) * (1.0 + scale) + shift


def _dot_nt(a, b):
    return lax.dot_general(a, b, (((1,), (1,)), ((), ())), preferred_element_type=F32)


def _mod_body(c_ref, w_ref, b_ref, o_ref):
    s = jax.nn.silu(c_ref[...])
    o_ref[0] = jnp.dot(s.astype(BF16), w_ref[0].astype(BF16), preferred_element_type=F32) + b_ref[0]


def _mod_vectors(cvec, w_mod, b_mod):
    depth, d, n = w_mod.shape
    r = cvec.shape[0]
    tn = _tile(n, 1536)
    vm = 2 * d * tn * 4 + d * tn * 2 + 4 * r * tn * 4
    return pl.pallas_call(
        _mod_body,
        out_shape=jax.ShapeDtypeStruct((depth, r, n), F32),
        grid=(depth, n // tn),
        in_specs=[
            pl.BlockSpec((r, d), lambda i, j: (0, 0)),
            pl.BlockSpec((1, d, tn), lambda i, j: (i, 0, j)),
            pl.BlockSpec((1, 1, tn), lambda i, j: (i, 0, j)),
        ],
        out_specs=pl.BlockSpec((1, r, tn), lambda i, j: (i, 0, j)),
        compiler_params=_params(("arbitrary", "arbitrary"), vm),
        name="mod_vectors",
    )(cvec, w_mod, b_mod.reshape(depth, 1, n))


def _rope(x, cos, sin_signed):
    lane = lax.broadcasted_iota(jnp.int32, x.shape, 1)
    qtr = HEAD_DIM // 4
    first = (lane % (2 * qtr)) < qtr
    rot = jnp.where(first, pltpu.roll(x, HEAD_DIM - qtr, 1), pltpu.roll(x, qtr, 1))
    return x * cos + rot * sin_signed


def _proj_body(*refs, tn, rope_cols):
    if rope_cols:
        x_ref, g_ref, sh_ref, sc_ref, w_ref, cos_ref, sin_ref, o_ref, h_scr = refs
    else:
        x_ref, g_ref, sh_ref, sc_ref, w_ref, o_ref, h_scr = refs
    h_scr[...] = _norm_mod(x_ref[...], g_ref[...], sh_ref[...], sc_ref[...]).astype(BF16)
    for j in range(w_ref.shape[1] // tn):
        ns = slice(j * tn, (j + 1) * tn)
        y = jnp.dot(h_scr[...], w_ref[:, ns], preferred_element_type=F32)
        if j * tn < rope_cols:
            cos = cos_ref[...]
            sin = sin_ref[...]
            y = jnp.concatenate(
                [_rope(y[:, k * HEAD_DIM:(k + 1) * HEAD_DIM], cos, sin)
                 for k in range(tn // HEAD_DIM)], axis=1)
        o_ref[:, ns] = y.astype(o_ref.dtype)


def _proj(x, gain, shift, scale, w, out_dtype, tn, rope=None):
    b, t, d = x.shape
    n = w.shape[1]
    tm = _tile(t, 512)
    in_specs = [
        pl.BlockSpec((None, tm, d), lambda bb, i: (bb, i, 0)),
        pl.BlockSpec((1, d), lambda bb, i: (0, 0)),
        pl.BlockSpec((None, 1, d), lambda bb, i: (bb, 0, 0)),
        pl.BlockSpec((None, 1, d), lambda bb, i: (bb, 0, 0)),
        pl.BlockSpec((d, n), lambda bb, i: (0, 0), pipeline_mode=pl.Buffered(1)),
    ]
    args = [x, gain.reshape(1, d), shift, scale, w]
    rope_cols = 0
    if rope is not None:
        cos, sin_signed, rope_cols = rope
        assert rope_cols % tn == 0
        in_specs += [pl.BlockSpec((tm, HEAD_DIM), lambda bb, i: (i, 0))] * 2
        args += [cos, sin_signed]
    osz = jnp.dtype(out_dtype).itemsize
    vm = 2 * tm * d * 4 + d * n * 2 + 2 * tm * n * osz + tm * d * 2 + 3 * tm * tn * 4
    return pl.pallas_call(
        functools.partial(_proj_body, tn=tn, rope_cols=rope_cols),
        out_shape=jax.ShapeDtypeStruct((b, t, n), out_dtype),
        grid=(b, t // tm),
        in_specs=in_specs,
        out_specs=pl.BlockSpec((None, tm, n), lambda bb, i: (bb, i, 0)),
        scratch_shapes=[pltpu.VMEM((tm, d), BF16)],
        compiler_params=_params(("arbitrary", "arbitrary"), vm),
        name="norm_mod_proj",
    )(*args)


def _oproj_body(a_ref, w_ref, res_ref, gate_ref, o_ref):
    y = jnp.dot(a_ref[...], w_ref[...], preferred_element_type=F32)
    o_ref[...] = res_ref[...] + gate_ref[...] * y


def _rec_oproj_body(yf_ref, yb_ref, gu_ref, w_ref, res_ref, gate_ref, o_ref):
    a = (yf_ref[...] + yb_ref[...]) * jax.nn.gelu(gu_ref[...])
    y = jnp.dot(a.astype(BF16), w_ref[...], preferred_element_type=F32)
    o_ref[...] = res_ref[...] + gate_ref[...] * y


def _oproj(a, w, res, gate):
    b, t, k = a.shape
    d = w.shape[1]
    tm = _tile(t, 512)
    vm = 2 * tm * k * 2 + 2 * k * d * 2 + 4 * tm * d * 4 + tm * d * 4
    return pl.pallas_call(
        _oproj_body,
        out_shape=jax.ShapeDtypeStruct((b, t, d), F32),
        grid=(b, t // tm),
        in_specs=[
            pl.BlockSpec((None, tm, k), lambda bb, i: (bb, i, 0)),
            pl.BlockSpec((k, d), lambda bb, i: (0, 0)),
            pl.BlockSpec((None, tm, d), lambda bb, i: (bb, i, 0)),
            pl.BlockSpec((None, 1, d), lambda bb, i: (bb, 0, 0)),
        ],
        out_specs=pl.BlockSpec((None, tm, d), lambda bb, i: (bb, i, 0)),
        compiler_params=_params(("arbitrary", "arbitrary"), vm),
        name="oproj_residual",
    )(a, w, res, gate)


def _rec_oproj(yf, yb, gu, w, res, gate):
    b, t, k = yf.shape
    d = w.shape[1]
    tm = _tile(t, 256)
    vm = 6 * tm * k * 4 + 2 * k * d * 2 + 4 * tm * d * 4 + 3 * tm * d * 4
    return pl.pallas_call(
        _rec_oproj_body,
        out_shape=jax.ShapeDtypeStruct((b, t, d), F32),
        grid=(b, t // tm),
        in_specs=[
            pl.BlockSpec((None, tm, k), lambda bb, i: (bb, i, 0)),
            pl.BlockSpec((None, tm, k), lambda bb, i: (bb, i, 0)),
            pl.BlockSpec((None, tm, k), lambda bb, i: (bb, i, 0)),
            pl.BlockSpec((k, d), lambda bb, i: (0, 0)),
            pl.BlockSpec((None, tm, d), lambda bb, i: (bb, i, 0)),
            pl.BlockSpec((None, 1, d), lambda bb, i: (bb, 0, 0)),
        ],
        out_specs=pl.BlockSpec((None, tm, d), lambda bb, i: (bb, i, 0)),
        compiler_params=_params(("arbitrary", "arbitrary"), vm),
        name="rec_oproj_residual",
    )(yf, yb, gu, w, res, gate)


def _stack_groups(q):
    return jnp.concatenate(
        [q[:, g * HEAD_DIM:(g + 1) * HEAD_DIM] for g in range(GQA_GROUP)], axis=0)


def _unstack_groups(o, rows):
    return jnp.concatenate([o[g * rows:(g + 1) * rows] for g in range(GQA_GROUP)], axis=1)


def _sink_column(sink_ref, h, rows):
    return jnp.concatenate(
        [jnp.full((rows, 1), sink_ref[h * GQA_GROUP + g], F32) for g in range(GQA_GROUP)], axis=0)


def _attn_body(sink_ref, q_ref, kp_ref, kc_ref, kn_ref, vp_ref, vc_ref, vn_ref, kx_ref, vx_ref,
               o_ref, *, nb):
    i = pl.program_id(1)
    blk = ATTN_BLOCK
    scale = HEAD_DIM ** -0.5
    r = lax.broadcasted_iota(jnp.int32, (GQA_GROUP * blk, blk), 0) % blk
    c = lax.broadcasted_iota(jnp.int32, (GQA_GROUP * blk, blk), 1)
    keep_p = (c >= r) & (i >= 1)
    keep_n = (c <= r) & (i + 1 < nb)
    neg = -jnp.inf
    qw = GQA_GROUP * HEAD_DIM
    for h in range(N_KV_HEADS):
        hs = slice(h * HEAD_DIM, (h + 1) * HEAD_DIM)
        qs = _stack_groups(q_ref[:, h * qw:(h + 1) * qw])
        s_p = jnp.where(keep_p, _dot_nt(qs, kp_ref[:, hs]) * scale, neg)
        s_c = _dot_nt(qs, kc_ref[:, hs]) * scale
        s_n = jnp.where(keep_n, _dot_nt(qs, kn_ref[:, hs]) * scale, neg)
        s_x = _dot_nt(qs, kx_ref[:, hs]) * scale
        sink = _sink_column(sink_ref, h, blk)
        m = jnp.maximum(
            jnp.maximum(jnp.maximum(s_p.max(-1, keepdims=True), s_c.max(-1, keepdims=True)),
                        jnp.maximum(s_n.max(-1, keepdims=True), s_x.max(-1, keepdims=True))),
            sink)
        p_p = jnp.exp(s_p - m)
        p_c = jnp.exp(s_c - m)
        p_n = jnp.exp(s_n - m)
        p_x = jnp.exp(s_x - m)
        denom = (p_p.sum(-1, keepdims=True) + p_c.sum(-1, keepdims=True)
                 + p_n.sum(-1, keepdims=True) + p_x.sum(-1, keepdims=True) + jnp.exp(sink - m))
        o = (jnp.dot(p_p.astype(BF16), vp_ref[:, hs], preferred_element_type=F32)
             + jnp.dot(p_c.astype(BF16), vc_ref[:, hs], preferred_element_type=F32)
             + jnp.dot(p_n.astype(BF16), vn_ref[:, hs], preferred_element_type=F32)
             + jnp.dot(p_x.astype(BF16), vx_ref[:, hs], preferred_element_type=F32))
        o_ref[:, h * qw:(h + 1) * qw] = _unstack_groups(o / denom, blk).astype(o_ref.dtype)


def _attention(qkv_x, qkv_c, sink):
    b, s, _ = qkv_x.shape
    c = qkv_c.shape[1]
    nb = s // ATTN_BLOCK
    q_cols = N_Q_HEADS * HEAD_DIM
    kv_cols = N_KV_HEADS * HEAD_DIM
    k0 = q_cols // kv_cols
    blk = ATTN_BLOCK

    def kv_spec(col, off):
        return pl.BlockSpec((None, blk, kv_cols),
                            lambda bb, i: (bb, jnp.clip(i + off, 0, nb - 1), col))

    in_specs = [
        pl.BlockSpec(memory_space=pltpu.SMEM),
        pl.BlockSpec((None, blk, q_cols), lambda bb, i: (bb, i, 0)),
        kv_spec(k0, -1), kv_spec(k0, 0), kv_spec(k0, 1),
        kv_spec(k0 + 1, -1), kv_spec(k0 + 1, 0), kv_spec(k0 + 1, 1),
        pl.BlockSpec((None, c, kv_cols), lambda bb, i: (bb, 0, k0)),
        pl.BlockSpec((None, c, kv_cols), lambda bb, i: (bb, 0, k0 + 1)),
    ]
    return pl.pallas_call(
        functools.partial(_attn_body, nb=nb),
        out_shape=jax.ShapeDtypeStruct((b, s, q_cols), BF16),
        grid=(b, nb),
        in_specs=in_specs,
        out_specs=pl.BlockSpec((None, blk, q_cols), lambda bb, i: (bb, i, 0)),
        compiler_params=_params(("arbitrary", "arbitrary"), 24 * 1024 * 1024),
        name="window_attention",
    )(sink, qkv_x, qkv_x, qkv_x, qkv_x, qkv_x, qkv_x, qkv_x, qkv_c, qkv_c)


def _ctx_attn_body(sink_ref, q_ref, k_ref, v_ref, o_ref):
    h = pl.program_id(1)
    rows = q_ref.shape[0]
    scale = HEAD_DIM ** -0.5
    qs = _stack_groups(q_ref[...])
    s = _dot_nt(qs, k_ref[...]) * scale
    sink = _sink_column(sink_ref, h, rows)
    m = jnp.maximum(s.max(-1, keepdims=True), sink)
    p = jnp.exp(s - m)
    denom = p.sum(-1, keepdims=True) + jnp.exp(sink - m)
    o = jnp.dot((p / denom).astype(BF16), v_ref[...], preferred_element_type=F32)
    o_ref[...] = _unstack_groups(o, rows).astype(o_ref.dtype)


def _ctx_attention(qkv_c, sink):
    b, c, _ = qkv_c.shape
    qw = GQA_GROUP * HEAD_DIM
    k0 = N_Q_HEADS
    v0 = N_Q_HEADS + N_KV_HEADS
    return pl.pallas_call(
        _ctx_attn_body,
        out_shape=jax.ShapeDtypeStruct((b, c, N_Q_HEADS * HEAD_DIM), BF16),
        grid=(b, N_KV_HEADS),
        in_specs=[
            pl.BlockSpec(memory_space=pltpu.SMEM),
            pl.BlockSpec((None, c, qw), lambda bb, h: (bb, 0, h)),
            pl.BlockSpec((None, c, HEAD_DIM), lambda bb, h: (bb, 0, k0 + h)),
            pl.BlockSpec((None, c, HEAD_DIM), lambda bb, h: (bb, 0, v0 + h)),
        ],
        out_specs=pl.BlockSpec((None, c, qw), lambda bb, h: (bb, 0, h)),
        compiler_params=_params(("arbitrary", "arbitrary"), 16 * 1024 * 1024),
        name="context_attention",
    )(sink, qkv_c, qkv_c, qkv_c)


def _block_diag(ub, w_ref, bias):
    bw = w_ref.shape[1]
    return jnp.concatenate(
        [jnp.dot(ub[:, n * bw:(n + 1) * bw], w_ref[n], preferred_element_type=F32)
         for n in range(w_ref.shape[0])], axis=1) + bias


def _lru_body(up_ref, uc_ref, un_ref, cw_ref, cb_ref, wa_ref, ba_ref, wx_ref, bx_ref, lam_ref,
              h0_ref, y_ref, hl_ref, h_scr, a_scr, b_scr, *, nt, reverse):
    i = pl.program_id(1)
    ti = (nt - 1 - i) if reverse else i
    tm = uc_ref.shape[0]
    halo = up_ref.shape[0]

    @pl.when(i == 0)
    def _():
        h_scr[...] = h0_ref[...]

    prev = jnp.where(ti > 0, up_ref[...], 0.0)
    nxt = jnp.where(ti < nt - 1, un_ref[...], 0.0)
    ext = jnp.concatenate([prev, uc_ref[...], nxt], axis=0)
    u = cb_ref[...]
    for k in range(CONV_W):
        off = halo - CONV_LEFT + k
        u = u + ext[off:off + tm] * cw_ref[k:k + 1, :]

    ub = u.astype(BF16)
    r = jax.nn.sigmoid(_block_diag(ub, wa_ref, ba_ref[...]))
    ig = jax.nn.sigmoid(_block_diag(ub, wx_ref, bx_ref[...]))
    nl = -lam_ref[...]
    softplus = jnp.maximum(nl, 0.0) + jnp.log1p(jnp.exp(-jnp.abs(nl)))
    log_a = -LRU_C * r * softplus
    a_scr[...] = jnp.exp(log_a)
    b_scr[...] = jnp.sqrt(1.0 - jnp.exp(2.0 * log_a)) * ig * u

    ng = tm // SUBLANES
    row = lax.broadcasted_iota(jnp.int32, (SUBLANES, a_scr.shape[1]), 0)

    def group(g, h):
        gi = (ng - 1 - g) if reverse else g
        r0 = pl.multiple_of(gi * SUBLANES, SUBLANES)
        a = a_scr[pl.ds(r0, SUBLANES), :]
        bb = b_scr[pl.ds(r0, SUBLANES), :]
        for k in (1, 2, 4):
            if reverse:
                keep = row < SUBLANES - k
                shift = SUBLANES - k
            else:
                keep = row >= k
                shift = k
            a_sh = pltpu.roll(a, shift, 0)
            b_sh = pltpu.roll(bb, shift, 0)
            bb = bb + a * jnp.where(keep, b_sh, 0.0)
            a = a * jnp.where(keep, a_sh, 1.0)
        y = bb + a * h
        y_ref[pl.ds(r0, SUBLANES), :] = y
        return y[0:1, :] if reverse else y[SUBLANES - 1:SUBLANES, :]

    h_last = lax.fori_loop(0, ng, group, h_scr[...])
    h_scr[...] = h_last

    @pl.when(i == nt - 1)
    def _():
        hl_ref[...] = h_last


def _lru_scan(gu, conv_w, conv_b, w_a, b_a, w_x, b_x, lam, h0, reverse):
    b, t, d2 = gu.shape
    d = d2 // 2
    tm = _tile(t, 256)
    nt = t // tm
    halo = SUBLANES
    hb = tm // halo
    nh = t // halo

    def tmap(i):
        return (nt - 1 - i) if reverse else i

    vec = lambda: pl.BlockSpec((1, d), lambda bb, i: (0, 0))
    in_specs = [
        pl.BlockSpec((None, halo, d), lambda bb, i: (bb, jnp.maximum(tmap(i) * hb - 1, 0), 1)),
        pl.BlockSpec((None, tm, d), lambda bb, i: (bb, tmap(i), 1)),
        pl.BlockSpec((None, halo, d), lambda bb, i: (bb, jnp.minimum((tmap(i) + 1) * hb, nh - 1), 1)),
        pl.BlockSpec((CONV_W, d), lambda bb, i: (0, 0)),
        vec(),
        pl.BlockSpec(w_a.shape, lambda bb, i: (0, 0, 0)),
        vec(),
        pl.BlockSpec(w_x.shape, lambda bb, i: (0, 0, 0)),
        vec(),
        vec(),
        pl.BlockSpec((None, 1, d), lambda bb, i: (bb, 0, 0)),
    ]
    vm = 2 * (tm + 2 * halo) * d * 4 + 2 * tm * d * 4 + 2 * tm * d * 4 + 8 * tm * d * 4 + 4 * w_a.size * 2
    return pl.pallas_call(
        functools.partial(_lru_body, nt=nt, reverse=reverse),
        out_shape=(jax.ShapeDtypeStruct((b, t, d), F32), jax.ShapeDtypeStruct((b, 1, d), F32)),
        grid=(b, nt),
        in_specs=in_specs,
        out_specs=(pl.BlockSpec((None, tm, d), lambda bb, i: (bb, tmap(i), 0)),
                   pl.BlockSpec((None, 1, d), lambda bb, i: (bb, 0, 0))),
        scratch_shapes=[pltpu.VMEM((1, d), F32), pltpu.VMEM((tm, d), F32), pltpu.VMEM((tm, d), F32)],
        compiler_params=_params(("arbitrary", "arbitrary"), vm),
        name="rglru_bwd" if reverse else "rglru_fwd",
    )(gu, gu, gu, conv_w, conv_b.reshape(1, d), w_a, b_a.reshape(1, d), w_x, b_x.reshape(1, d),
      lam.reshape(1, d), h0)


def _peer_query_body(x_ref, g_ref, sh_ref, sc_ref, wq_ref, keys_ref, ht_ref, st_ref):
    h = _norm_mod(x_ref[...], g_ref[...], sh_ref[...], sc_ref[...])
    hb = h.astype(BF16)
    ht = h.T.astype(BF16)
    for p in range(ht_ref.shape[0]):
        ht_ref[p] = ht[:, p * PEER_PIECE:(p + 1) * PEER_PIECE]
    q = jnp.dot(hb, wq_ref[...], preferred_element_type=F32).astype(BF16)
    for hp in range(2 * PEER_HEADS):
        qc = q[:, hp * PEER_KEY_DIM:(hp + 1) * PEER_KEY_DIM]
        st_ref[hp] = _dot_nt(keys_ref[hp % 2], qc)


def _peer_query(x, gain, shift, scale, wq, keys):
    b, t, d = x.shape
    tm = _tile(t, PEER_SUB)
    nt = t // tm
    nq = wq.shape[1]
    vm = 2 * tm * d * 4 + 2 * d * nq * 2 + 2 * d * tm * 2 + 2 * 2 * PEER_HEADS * N_KEYS * tm * 4 + 4 * tm * d * 4
    return pl.pallas_call(
        _peer_query_body,
        out_shape=(jax.ShapeDtypeStruct((b * nt, tm // PEER_PIECE, d, PEER_PIECE), BF16),
                   jax.ShapeDtypeStruct((2 * PEER_HEADS, N_KEYS, b * t), F32)),
        grid=(b, nt),
        in_specs=[
            pl.BlockSpec((None, tm, d), lambda bb, i: (bb, i, 0)),
            pl.BlockSpec((1, d), lambda bb, i: (0, 0)),
            pl.BlockSpec((None, 1, d), lambda bb, i: (bb, 0, 0)),
            pl.BlockSpec((None, 1, d), lambda bb, i: (bb, 0, 0)),
            pl.BlockSpec((d, nq), lambda bb, i: (0, 0)),
            pl.BlockSpec(keys.shape, lambda bb, i: (0, 0, 0)),
        ],
        out_specs=(pl.BlockSpec((None, tm // PEER_PIECE, d, PEER_PIECE),
                                lambda bb, i: (bb * nt + i, 0, 0, 0)),
                   pl.BlockSpec((2 * PEER_HEADS, N_KEYS, tm), lambda bb, i: (0, 0, bb * nt + i))),
        compiler_params=_params(("arbitrary", "arbitrary"), vm),
        name="peer_query",
    )(x, gain.reshape(1, d), shift, scale, wq, keys)


def _oddeven_merge(lo, hi, r):
    step = r * 2
    if step < hi - lo:
        yield from _oddeven_merge(lo, hi, step)
        yield from _oddeven_merge(lo + r, hi, step)
        yield from [(k, k + r) for k in range(lo + r, hi - r, step)]
    else:
        yield (lo, lo + r)


def _oddeven_sort(lo, hi):
    if hi - lo >= 1:
        mid = lo + (hi - lo) // 2
        yield from _oddeven_sort(lo, mid)
        yield from _oddeven_sort(mid + 1, hi)
        yield from _oddeven_merge(lo, hi, 1)


_SORT16 = tuple(_oddeven_sort(0, PEER_TOPK - 1))


def _exchange(x, p, q):
    hi = jnp.maximum(x[p], x[q])
    lo = jnp.minimum(x[p], x[q])
    x[p] = hi
    x[q] = lo


def _merge_sublanes(x):
    n = len(x)
    shift = SUBLANES // 2
    while shift >= 1:
        z = [jnp.maximum(x[k], pltpu.roll(x[n - 1 - k], shift, 0)) for k in range(n)]
        dist = n // 2
        while dist >= 1:
            for k in range(n):
                if k & dist == 0:
                    _exchange(z, k, k + dist)
            dist //= 2
        x = z
        shift //= 2
    return x


def _top16_sorted(s):
    x = [s[SUBLANES * v:SUBLANES * (v + 1)] for v in range(s.shape[0] // SUBLANES)]
    assert len(x) == PEER_TOPK
    for p, q in _SORT16:
        _exchange(x, p, q)
    return _merge_sublanes(x)


def _dup16(v):
    bits = pltpu.bitcast(v.astype(BF16).astype(F32), jnp.uint32) >> 16
    return bits | (bits << 16)


def _route_stats(st_ref, e1d_scr, c1d_scr, e2_scr, r2_scr):
    tm = st_ref.shape[2]
    row = lax.broadcasted_iota(jnp.int32, (SUBLANES, tm), 0)
    for h in range(PEER_HEADS):
        s1 = st_ref[2 * h]
        s2 = st_ref[2 * h + 1]
        a = _top16_sorted(s1)
        b = _top16_sorted(s2)
        a_lo = a[SUBLANES - 1]
        a_hi = a[2 * SUBLANES - 1]
        for i in range(SUBLANES - 2, -1, -1):
            a_lo = jnp.where(row == i, a[i], a_lo)
            a_hi = jnp.where(row == i, a[SUBLANES + i], a_hi)
        c = [a_lo + b[j] for j in range(PEER_TOPK)]
        d = a_hi + b[0]
        c = [jnp.maximum(c[0], d)] + [
            jnp.maximum(c[j], jnp.minimum(c[j - 1], d)) for j in range(1, PEER_TOPK)]
        t = _merge_sublanes(c)
        z = jnp.ones_like(t[0])
        for k in range(1, PEER_TOPK):
            z = z + jnp.exp(t[k] - t[0])
        inv_z = 1.0 / z
        tau = t[PEER_TOPK - 1][0:1]
        count1 = jnp.zeros_like(s1)
        rank2 = jnp.zeros_like(s2)
        for j in range(PEER_TOPK):
            bj = b[j][0:1]
            count1 = count1 + jnp.where(s1 + bj >= tau, 1.0, 0.0)
            rank2 = rank2 + jnp.where(bj > s2, 1.0, 0.0)
        e1 = jnp.exp(s1 - a[0][0:1]) * inv_z[0:1]
        e2 = jnp.exp(s2 - b[0][0:1])
        for cc in range(tm // LANES):
            cs = slice(cc * LANES, (cc + 1) * LANES)
            c1d_scr[h, cc] = _dup16(count1[:, cs])
            e1d_scr[h, cc] = _dup16(e1[:, cs])
            r2_scr[h, cc] = rank2[:, cs].astype(BF16)
            e2_scr[h, cc] = e2[:, cs].astype(BF16)


def _route_body(st_ref, e1d_ref, c1d_ref, e2_ref, r2_ref):
    _route_stats(st_ref, e1d_ref, c1d_ref, e2_ref, r2_ref)


def _peer_route(st):
    t = st.shape[2]
    tm = _tile(t, PEER_SUB)
    nc = tm // LANES
    shape = (PEER_HEADS, t // LANES, N_KEYS, LANES)
    spec = pl.BlockSpec((PEER_HEADS, nc, N_KEYS, LANES), lambda i: (0, i, 0, 0))
    stat = PEER_HEADS * N_KEYS * tm
    return pl.pallas_call(
        _route_body,
        out_shape=(jax.ShapeDtypeStruct(shape, jnp.uint32), jax.ShapeDtypeStruct(shape, jnp.uint32),
                   jax.ShapeDtypeStruct(shape, BF16), jax.ShapeDtypeStruct(shape, BF16)),
        grid=(t // tm,),
        in_specs=[pl.BlockSpec((2 * PEER_HEADS, N_KEYS, tm), lambda i: (0, 0, i))],
        out_specs=(spec, spec, spec, spec),
        compiler_params=_params(("arbitrary",), 2 * 2 * stat * 4 + 2 * stat * 12 + 8 * stat),
        name="peer_route",
    )(st)


def _dense_act(zt_ref, act_ref, blk, c0, cols, e1d_ref, c1d_ref, e2_ref, r2_ref):
    eb = zt_ref.shape[1]
    for l in range(eb // N_KEYS):
        i1 = jnp.clip(blk * (eb // N_KEYS) + l, 0, N_KEYS - 1)
        rs = slice(l * N_KEYS, (l + 1) * N_KEYS)
        for c in cols:
            g = None
            for h in range(PEER_HEADS):
                cnt = jnp.broadcast_to(c1d_ref[h, c0 + c, pl.ds(i1, 1), :], (N_KEYS // 2, LANES))
                e1 = jnp.broadcast_to(e1d_ref[h, c0 + c, pl.ds(i1, 1), :], (N_KEYS // 2, LANES))
                gh = jnp.where(r2_ref[h, c0 + c] < pltpu.bitcast(cnt, BF16),
                               e2_ref[h, c0 + c] * pltpu.bitcast(e1, BF16), jnp.zeros((), BF16))
                g = gh if g is None else g + gh
            act_ref[c, rs, :] = jax.nn.gelu(zt_ref[c, rs, :].astype(BF16)) * g


def _peer_expert_body(ht_ref, e1d_hbm, c1d_hbm, e2_hbm, r2_hbm, u_ref, vt_ref, o_ref,
                      e1d_scr, c1d_scr, e2_scr, r2_scr, gate_sem, zt_scr, act_scr, *, eb):
    j = pl.program_id(1)
    gates = (e1d_scr, c1d_scr, e2_scr, r2_scr)

    @pl.when(j == 0)
    def _():
        gnc = e1d_scr.shape[1]
        first = pl.multiple_of(pl.program_id(0) * gnc, gnc)
        copies = [
            pltpu.make_async_copy(src.at[:, pl.ds(first, gnc)], dst, gate_sem.at[n])
            for n, (src, dst) in enumerate(zip((e1d_hbm, c1d_hbm, e2_hbm, r2_hbm), gates))]
        for cp in copies:
            cp.start()
        o_ref[...] = jnp.zeros_like(o_ref)
        zt_scr[...] = jnp.zeros_like(zt_scr)
        act_scr[...] = jnp.zeros_like(act_scr)
        for cp in copies:
            cp.wait()

    n_sub, n_pc, _, pw = ht_ref.shape
    cpp = pw // LANES
    nc = n_pc * cpp

    def sub_tile(k, carry):
        for s in (0, 1):
            for p in range(n_pc):
                cols = range(p * cpp, (p + 1) * cpp)
                act = jnp.concatenate([act_scr[k, s, c] for c in cols], axis=1)
                o_ref[k, p] += jnp.dot(vt_ref[s], act, preferred_element_type=F32)
                _dense_act(zt_scr.at[k, 1 - s], act_scr.at[k, 1 - s], 2 * j - 1 + s, k * nc, cols,
                           *gates)
                z = jnp.dot(u_ref[s * eb:(s + 1) * eb, :], ht_ref[k, p],
                            preferred_element_type=F32)
                for n, c in enumerate(cols):
                    zt_scr[k, s, c] = z[:, n * LANES:(n + 1) * LANES]
        return carry

    lax.fori_loop(0, n_sub, sub_tile, 0)


def _peer_experts(ht, gates, u, vt):
    n_tiles, n_pc, d, pw = ht.shape
    tm = n_pc * pw
    e = u.shape[0]
    g = 2 if n_tiles % 2 == 0 else 1
    nc = tm // LANES
    eb = PEER_EB
    ns = e // (2 * eb)
    assert vt.shape == (ns, 2, d, eb)
    once = pl.Buffered(1)
    gate_spec = pl.BlockSpec(memory_space=pl.ANY)
    gate_shape = (PEER_HEADS, g * nc, N_KEYS, LANES)
    in_specs = [
        pl.BlockSpec((g, n_pc, d, pw), lambda i, j: (i, 0, 0, 0), pipeline_mode=once),
        gate_spec, gate_spec, gate_spec, gate_spec,
        pl.BlockSpec((2 * eb, d), lambda i, j: (jnp.minimum(j, ns - 1), 0)),
        pl.BlockSpec((None, 2, d, eb), lambda i, j: (jnp.maximum(j - 1, 0), 0, 0, 0)),
    ]
    stat = PEER_HEADS * N_KEYS * tm
    vm = g * (d * tm * 2 + stat * 12 + 2 * d * tm * 4 + 2 * eb * tm * 6) + 8 * eb * d * 2
    return pl.pallas_call(
        functools.partial(_peer_expert_body, eb=eb),
        out_shape=jax.ShapeDtypeStruct((n_tiles, n_pc, d, pw), F32),
        grid=(n_tiles // g, ns + 1),
        in_specs=in_specs,
        out_specs=pl.BlockSpec((g, n_pc, d, pw), lambda i, j: (i, 0, 0, 0)),
        scratch_shapes=[
            pltpu.VMEM(gate_shape, jnp.uint32),
            pltpu.VMEM(gate_shape, jnp.uint32),
            pltpu.VMEM(gate_shape, BF16),
            pltpu.VMEM(gate_shape, BF16),
            pltpu.SemaphoreType.DMA((4,)),
            pltpu.VMEM((g, 2, nc, eb, LANES), F32),
            pltpu.VMEM((g, 2, nc, eb, LANES), BF16),
        ],
        compiler_params=_params(("arbitrary", "arbitrary"), vm),
        name="peer_experts",
    )(ht, *gates, u, vt)


def _peer_residual_body(*refs, final_norm):
    if final_norm:
        xs_ref, gate_ref, ft_ref, gain_ref, o_ref = refs
    else:
        xs_ref, gate_ref, ft_ref, o_ref = refs
    f = jnp.concatenate([ft_ref[p].T for p in range(ft_ref.shape[0])], axis=0)
    y = xs_ref[...] + gate_ref[...] * f
    if final_norm:
        ms = jnp.mean(y * y, axis=-1, keepdims=True)
        y = (y * lax.rsqrt(ms + NORM_EPS)) * gain_ref[...]
    o_ref[...] = y


def _peer_residual(xs, gate, ft, final_gain=None):
    b, t, d = xs.shape
    n_pc, pw = ft.shape[1], ft.shape[3]
    tm = n_pc * pw
    nt = t // tm
    final_norm = final_gain is not None
    in_specs = [
        pl.BlockSpec((None, tm, d), lambda bb, i: (bb, i, 0)),
        pl.BlockSpec((None, 1, d), lambda bb, i: (bb, 0, 0)),
        pl.BlockSpec((None, n_pc, d, pw), lambda bb, i: (bb * nt + i, 0, 0, 0)),
    ]
    args = [xs, gate, ft]
    if final_norm:
        in_specs.append(pl.BlockSpec((1, d), lambda bb, i: (0, 0)))
        args.append(final_gain.reshape(1, d))
    return pl.pallas_call(
        functools.partial(_peer_residual_body, final_norm=final_norm),
        out_shape=jax.ShapeDtypeStruct((b, t, d), F32),
        grid=(b, nt),
        in_specs=in_specs,
        out_specs=pl.BlockSpec((None, tm, d), lambda bb, i: (bb, i, 0)),
        compiler_params=_params(("arbitrary", "arbitrary"), 8 * tm * d * 4),
        name="peer_residual",
    )(*args)


def _tables_body(u_ref, v_ref, ub_ref, vt_ref):
    ub_ref[...] = u_ref[...].astype(BF16)
    vt_ref[...] = v_ref[...].T.astype(BF16)


def _peer_tables(u, v):
    e, d = u.shape
    eb = PEER_EB
    return pl.pallas_call(
        _tables_body,
        out_shape=(jax.ShapeDtypeStruct((e, d), BF16),
                   jax.ShapeDtypeStruct((e // (2 * eb), 2, d, eb), BF16)),
        grid=(e // eb,),
        in_specs=[pl.BlockSpec((eb, d), lambda i: (i, 0)), pl.BlockSpec((eb, d), lambda i: (i, 0))],
        out_specs=(pl.BlockSpec((eb, d), lambda i: (i, 0)),
                   pl.BlockSpec((None, None, d, eb), lambda i: (i // 2, i % 2, 0, 0))),
        compiler_params=_params(("arbitrary",), 2 * 2 * eb * d * 4 + 2 * 2 * eb * d * 2 + 2 * eb * d * 4),
        name="peer_tables",
    )(u, v)


def _peer(xs, gain, shift, scale, gate, wq, keys, u, vt, final_gain=None):
    ht, st = _peer_query(xs, gain, shift, scale, wq, keys)
    ft = _peer_experts(ht, _peer_route(st), u, vt)
    return _peer_residual(xs, gate, ft, final_gain)


def _rope_tables(s):
    t = jnp.arange(s)
    row = (t // GRID_W).astype(F32)
    col = (t % GRID_W).astype(F32)
    half = HEAD_DIM // 2
    inv = ROPE_BASE ** (-jnp.arange(0, half, 2, dtype=F32) / half)
    ang_r = row[:, None] * inv[None, :]
    ang_c = col[:, None] * inv[None, :]
    ang = jnp.concatenate([ang_r, ang_r, ang_c, ang_c], axis=-1)
    lane = jnp.arange(HEAD_DIM)
    sign = jnp.where((lane % half) < half // 2, -1.0, 1.0).astype(F32)
    return jnp.cos(ang), jnp.sin(ang) * sign[None, :]


def kernel(x, c, ctx, c_ctx, w_mod, b_mod, norm_mix, norm_ffn, norm_final, attn_w_qkv, attn_w_o, attn_sink, rec_w_in, rec_conv_w, rec_conv_b, rec_w_a, rec_b_a, rec_w_x, rec_b_x, rec_lambda, rec_w_out, peer_w_q, peer_keys, peer_u, peer_v):
    b, s, d = x.shape
    cl = ctx.shape[1]
    depth = w_mod.shape[0]
    rows = -(-(b + 1) // SUBLANES) * SUBLANES
    cvec = jnp.zeros((rows, d), F32).at[:b].set(c).at[b].set(c_ctx)
    mod = _mod_vectors(cvec, w_mod, b_mod).reshape(depth, rows, N_MOD, d)
    cos, sin_signed = _rope_tables(s)
    q_cols = N_Q_HEADS * HEAD_DIM
    kv_cols = N_KV_HEADS * HEAD_DIM

    xs = x
    cs = ctx.reshape(1, b * cl, d)
    for i in range(depth):
        last = i == depth - 1
        j = i // 2
        mx = [mod[i, :b, k][:, None, :] for k in range(N_MOD)]
        mc = [mod[i, b:b + 1, k][:, None, :] for k in range(N_MOD)]

        if i % 2 == 0:
            wqkv = attn_w_qkv[j].astype(BF16)
            wo = attn_w_o[j].astype(BF16)
            qkv_x = _proj(xs, norm_mix[i], mx[0], mx[1], wqkv, BF16, 512,
                          rope=(cos, sin_signed, q_cols + kv_cols))
            qkv_c = _proj(cs, norm_mix[i], mc[0], mc[1], wqkv, BF16, 512).reshape(b, cl, -1)
            o_x = _attention(qkv_x, qkv_c, attn_sink[j])
            xs = _oproj(o_x, wo, xs, mx[2])
            if not last:
                o_c = _ctx_attention(qkv_c, attn_sink[j]).reshape(1, b * cl, q_cols)
                cs = _oproj(o_c, wo, cs, mc[2])
        else:
            w_in = rec_w_in[j].astype(BF16)
            w_out = rec_w_out[j].astype(BF16)
            w_a = rec_w_a[j].astype(BF16)
            w_x = rec_w_x[j].astype(BF16)
            gu_x = _proj(xs, norm_mix[i], mx[0], mx[1], w_in, F32, 1024)
            gu_c = _proj(cs, norm_mix[i], mc[0], mc[1], w_in, F32, 1024).reshape(b, cl, -1)
            h0 = jnp.zeros((b, 1, d), F32)
            ys_c, ys_x = [], []
            for r in range(2):
                lru = functools.partial(
                    _lru_scan, conv_w=rec_conv_w[j], conv_b=rec_conv_b[j], w_a=w_a[r],
                    b_a=rec_b_a[j, r], w_x=w_x[r], b_x=rec_b_x[j, r], lam=rec_lambda[j, r],
                    reverse=(r == 1))
                y_c, h_c = lru(gu_c, h0=h0)
                y_x, _ = lru(gu_x, h0=h_c)
                ys_c.append(y_c)
                ys_x.append(y_x)
            xs = _rec_oproj(ys_x[0], ys_x[1], gu_x, w_out, xs, mx[2])
            if not last:
                cs = _rec_oproj(ys_c[0].reshape(1, b * cl, d), ys_c[1].reshape(1, b * cl, d),
                                gu_c.reshape(1, b * cl, -1), w_out, cs, mc[2])

        wq = peer_w_q[i].astype(BF16)
        keys = peer_keys[i].astype(BF16)
        u, vt = _peer_tables(peer_u[i], peer_v[i])
        xs = _peer(xs, norm_ffn[i], mx[3], mx[4], mx[5], wq, keys, u, vt,
                   final_gain=norm_final if last else None)
        if not last:
            cs = _peer(cs, norm_ffn[i], mc[3], mc[4], mc[5], wq, keys, u, vt)
    return xs
```

```python
import functools
import math

import jax
import jax.numpy as jnp
from jax import lax
from jax.experimental import pallas as pl
from jax.experimental.pallas import tpu as pltpu

F32 = jnp.float32
BF16 = jnp.bfloat16

NORM_EPS = 1e-6
N_MOD = 6
GRID_W = 64
HEAD_DIM = 128
N_Q_HEADS = 16
N_KV_HEADS = 4
GQA_GROUP = N_Q_HEADS // N_KV_HEADS
ATTN_BLOCK = 128
ROPE_BASE = 10000.0
RNN_BLOCKS = 8
CONV_W = 4
CONV_LEFT = 1
LRU_C = 8.0
PEER_HEADS = 8
PEER_KEY_DIM = 128
N_KEYS = 128
PEER_TOPK = 16
PEER_EB = 512
PEER_SUB = 512
PEER_PIECE = 256

V7X_VMEM_BYTES = 64 * 1024 * 1024
SUBLANES = 8
LANES = 128
VMEM_CAP = V7X_VMEM_BYTES - 6 * 1024 * 1024


def _params(sem, vmem_bytes):
    limit = int(min(VMEM_CAP, max(32 * 1024 * 1024, vmem_bytes * 5 // 4)))
    return pltpu.CompilerParams(dimension_semantics=sem, vmem_limit_bytes=limit)


def _tile(n, pref):
    return pref if n % pref == 0 else n


def _norm_mod(x, gain, shift, scale):
    ms = jnp.mean(x * x, axis=-1, keepdims=True)
    y = x * lax.rsqrt(ms + NORM_EPS)
    return (y * gain) * (1.0 + scale) + shift


def _dot_nt(a, b):
    return lax.dot_general(a, b, (((1,), (1,)), ((), ())), preferred_element_type=F32)


def _mod_body(c_ref, w_ref, b_ref, o_ref):
    s = jax.nn.silu(c_ref[...])
    o_ref[0] = jnp.dot(s.astype(BF16), w_ref[0].astype(BF16), preferred_element_type=F32) + b_ref[0]


def _mod_vectors(cvec, w_mod, b_mod):
    depth, d, n = w_mod.shape
    r = cvec.shape[0]
    tn = _tile(n, 1536)
    vm = 2 * d * tn * 4 + d * tn * 2 + 4 * r * tn * 4
    return pl.pallas_call(
        _mod_body,
        out_shape=jax.ShapeDtypeStruct((depth, r, n), F32),
        grid=(depth, n // tn),
        in_specs=[
            pl.BlockSpec((r, d), lambda i, j: (0, 0)),
            pl.BlockSpec((1, d, tn), lambda i, j: (i, 0, j)),
            pl.BlockSpec((1, 1, tn), lambda i, j: (i, 0, j)),
        ],
        out_specs=pl.BlockSpec((1, r, tn), lambda i, j: (i, 0, j)),
        compiler_params=_params(("arbitrary", "arbitrary"), vm),
        name="mod_vectors",
    )(cvec, w_mod, b_mod.reshape(depth, 1, n))


def _rope(x, cos, sin_signed):
    lane = lax.broadcasted_iota(jnp.int32, x.shape, 1)
    qtr = HEAD_DIM // 4
    first = (lane % (2 * qtr)) < qtr
    rot = jnp.where(first, pltpu.roll(x, HEAD_DIM - qtr, 1), pltpu.roll(x, qtr, 1))
    return x * cos + rot * sin_signed


def _proj_body(*refs, tn, rope_cols):
    if rope_cols:
        x_ref, g_ref, sh_ref, sc_ref, w_ref, cos_ref, sin_ref, o_ref, h_scr = refs
    else:
        x_ref, g_ref, sh_ref, sc_ref, w_ref, o_ref, h_scr = refs
    h_scr[...] = _norm_mod(x_ref[...], g_ref[...], sh_ref[...], sc_ref[...]).astype(BF16)
    for j in range(w_ref.shape[1] // tn):
        ns = slice(j * tn, (j + 1) * tn)
        y = jnp.dot(h_scr[...], w_ref[:, ns], preferred_element_type=F32)
        if j * tn < rope_cols:
            cos = cos_ref[...]
            sin = sin_ref[...]
            y = jnp.concatenate(
                [_rope(y[:, k * HEAD_DIM:(k + 1) * HEAD_DIM], cos, sin)
                 for k in range(tn // HEAD_DIM)], axis=1)
        o_ref[:, ns] = y.astype(o_ref.dtype)


def _proj(x, gain, shift, scale, w, out_dtype, tn, rope=None):
    b, t, d = x.shape
    n = w.shape[1]
    tm = _tile(t, 512)
    in_specs = [
        pl.BlockSpec((None, tm, d), lambda bb, i: (bb, i, 0)),
        pl.BlockSpec((1, d), lambda bb, i: (0, 0)),
        pl.BlockSpec((None, 1, d), lambda bb, i: (bb, 0, 0)),
        pl.BlockSpec((None, 1, d), lambda bb, i: (bb, 0, 0)),
        pl.BlockSpec((d, n), lambda bb, i: (0, 0), pipeline_mode=pl.Buffered(1)),
    ]
    args = [x, gain.reshape(1, d), shift, scale, w]
    rope_cols = 0
    if rope is not None:
        cos, sin_signed, rope_cols = rope
        assert rope_cols % tn == 0
        in_specs += [pl.BlockSpec((tm, HEAD_DIM), lambda bb, i: (i, 0))] * 2
        args += [cos, sin_signed]
    osz = jnp.dtype(out_dtype).itemsize
    vm = 2 * tm * d * 4 + d * n * 2 + 2 * tm * n * osz + tm * d * 2 + 3 * tm * tn * 4
    return pl.pallas_call(
        functools.partial(_proj_body, tn=tn, rope_cols=rope_cols),
        out_shape=jax.ShapeDtypeStruct((b, t, n), out_dtype),
        grid=(b, t // tm),
        in_specs=in_specs,
        out_specs=pl.BlockSpec((None, tm, n), lambda bb, i: (bb, i, 0)),
        scratch_shapes=[pltpu.VMEM((tm, d), BF16)],
        compiler_params=_params(("arbitrary", "arbitrary"), vm),
        name="norm_mod_proj",
    )(*args)


def _oproj_body(a_ref, w_ref, res_ref, gate_ref, o_ref):
    y = jnp.dot(a_ref[...], w_ref[...], preferred_element_type=F32)
    o_ref[...] = res_ref[...] + gate_ref[...] * y


def _rec_oproj_body(yf_ref, yb_ref, gu_ref, w_ref, res_ref, gate_ref, o_ref):
    a = (yf_ref[...] + yb_ref[...]) * jax.nn.gelu(gu_ref[...])
    y = jnp.dot(a.astype(BF16), w_ref[...], preferred_element_type=F32)
    o_ref[...] = res_ref[...] + gate_ref[...] * y


def _oproj(a, w, res, gate):
    b, t, k = a.shape
    d = w.shape[1]
    tm = _tile(t, 512)
    vm = 2 * tm * k * 2 + 2 * k * d * 2 + 4 * tm * d * 4 + tm * d * 4
    return pl.pallas_call(
        _oproj_body,
        out_shape=jax.ShapeDtypeStruct((b, t, d), F32),
        grid=(b, t // tm),
        in_specs=[
            pl.BlockSpec((None, tm, k), lambda bb, i: (bb, i, 0)),
            pl.BlockSpec((k, d), lambda bb, i: (0, 0)),
            pl.BlockSpec((None, tm, d), lambda bb, i: (bb, i, 0)),
            pl.BlockSpec((None, 1, d), lambda bb, i: (bb, 0, 0)),
        ],
        out_specs=pl.BlockSpec((None, tm, d), lambda bb, i: (bb, i, 0)),
        compiler_params=_params(("arbitrary", "arbitrary"), vm),
        name="oproj_residual",
    )(a, w, res, gate)


def _rec_oproj(yf, yb, gu, w, res, gate):
    b, t, k = yf.shape
    d = w.shape[1]
    tm = _tile(t, 256)
    vm = 6 * tm * k * 4 + 2 * k * d * 2 + 4 * tm * d * 4 + 3 * tm * d * 4
    return pl.pallas_call(
        _rec_oproj_body,
        out_shape=jax.ShapeDtypeStruct((b, t, d), F32),
        grid=(b, t // tm),
        in_specs=[
            pl.BlockSpec((None, tm, k), lambda bb, i: (bb, i, 0)),
            pl.BlockSpec((None, tm, k), lambda bb, i: (bb, i, 0)),
            pl.BlockSpec((None, tm, k), lambda bb, i: (bb, i, 0)),
            pl.BlockSpec((k, d), lambda bb, i: (0, 0)),
            pl.BlockSpec((None, tm, d), lambda bb, i: (bb, i, 0)),
            pl.BlockSpec((None, 1, d), lambda bb, i: (bb, 0, 0)),
        ],
        out_specs=pl.BlockSpec((None, tm, d), lambda bb, i: (bb, i, 0)),
        compiler_params=_params(("arbitrary", "arbitrary"), vm),
        name="rec_oproj_residual",
    )(yf, yb, gu, w, res, gate)


def _stack_groups(q):
    return jnp.concatenate(
        [q[:, g * HEAD_DIM:(g + 1) * HEAD_DIM] for g in range(GQA_GROUP)], axis=0)


def _unstack_groups(o, rows):
    return jnp.concatenate([o[g * rows:(g + 1) * rows] for g in range(GQA_GROUP)], axis=1)


def _sink_column(sink_ref, h, rows):
    return jnp.concatenate(
        [jnp.full((rows, 1), sink_ref[h * GQA_GROUP + g], F32) for g in range(GQA_GROUP)], axis=0)


def _attn_body(sink_ref, q_ref, kp_ref, kc_ref, kn_ref, vp_ref, vc_ref, vn_ref, kx_ref, vx_ref,
               o_ref, *, nb):
    i = pl.program_id(1)
    blk = ATTN_BLOCK
    scale = HEAD_DIM ** -0.5
    r = lax.broadcasted_iota(jnp.int32, (GQA_GROUP * blk, blk), 0) % blk
    c = lax.broadcasted_iota(jnp.int32, (GQA_GROUP * blk, blk), 1)
    keep_p = (c >= r) & (i >= 1)
    keep_n = (c <= r) & (i + 1 < nb)
    neg = -jnp.inf
    qw = GQA_GROUP * HEAD_DIM
    for h in range(N_KV_HEADS):
        hs = slice(h * HEAD_DIM, (h + 1) * HEAD_DIM)
        qs = _stack_groups(q_ref[:, h * qw:(h + 1) * qw])
        s_p = jnp.where(keep_p, _dot_nt(qs, kp_ref[:, hs]) * scale, neg)
        s_c = _dot_nt(qs, kc_ref[:, hs]) * scale
        s_n = jnp.where(keep_n, _dot_nt(qs, kn_ref[:, hs]) * scale, neg)
        s_x = _dot_nt(qs, kx_ref[:, hs]) * scale
        sink = _sink_column(sink_ref, h, blk)
        m = jnp.maximum(
            jnp.maximum(jnp.maximum(s_p.max(-1, keepdims=True), s_c.max(-1, keepdims=True)),
                        jnp.maximum(s_n.max(-1, keepdims=True), s_x.max(-1, keepdims=True))),
            sink)
        p_p = jnp.exp(s_p - m)
        p_c = jnp.exp(s_c - m)
        p_n = jnp.exp(s_n - m)
        p_x = jnp.exp(s_x - m)
        denom = (p_p.sum(-1, keepdims=True) + p_c.sum(-1, keepdims=True)
                 + p_n.sum(-1, keepdims=True) + p_x.sum(-1, keepdims=True) + jnp.exp(sink - m))
        o = (jnp.dot(p_p.astype(BF16), vp_ref[:, hs], preferred_element_type=F32)
             + jnp.dot(p_c.astype(BF16), vc_ref[:, hs], preferred_element_type=F32)
             + jnp.dot(p_n.astype(BF16), vn_ref[:, hs], preferred_element_type=F32)
             + jnp.dot(p_x.astype(BF16), vx_ref[:, hs], preferred_element_type=F32))
        o_ref[:, h * qw:(h + 1) * qw] = _unstack_groups(o / denom, blk).astype(o_ref.dtype)


def _attention(qkv_x, qkv_c, sink):
    b, s, _ = qkv_x.shape
    c = qkv_c.shape[1]
    nb = s // ATTN_BLOCK
    q_cols = N_Q_HEADS * HEAD_DIM
    kv_cols = N_KV_HEADS * HEAD_DIM
    k0 = q_cols // kv_cols
    blk = ATTN_BLOCK

    def kv_spec(col, off):
        return pl.BlockSpec((None, blk, kv_cols),
                            lambda bb, i: (bb, jnp.clip(i + off, 0, nb - 1), col))

    in_specs = [
        pl.BlockSpec(memory_space=pltpu.SMEM),
        pl.BlockSpec((None, blk, q_cols), lambda bb, i: (bb, i, 0)),
        kv_spec(k0, -1), kv_spec(k0, 0), kv_spec(k0, 1),
        kv_spec(k0 + 1, -1), kv_spec(k0 + 1, 0), kv_spec(k0 + 1, 1),
        pl.BlockSpec((None, c, kv_cols), lambda bb, i: (bb, 0, k0)),
        pl.BlockSpec((None, c, kv_cols), lambda bb, i: (bb, 0, k0 + 1)),
    ]
    return pl.pallas_call(
        functools.partial(_attn_body, nb=nb),
        out_shape=jax.ShapeDtypeStruct((b, s, q_cols), BF16),
        grid=(b, nb),
        in_specs=in_specs,
        out_specs=pl.BlockSpec((None, blk, q_cols), lambda bb, i: (bb, i, 0)),
        compiler_params=_params(("arbitrary", "arbitrary"), 24 * 1024 * 1024),
        name="window_attention",
    )(sink, qkv_x, qkv_x, qkv_x, qkv_x, qkv_x, qkv_x, qkv_x, qkv_c, qkv_c)


def _ctx_attn_body(sink_ref, q_ref, k_ref, v_ref, o_ref):
    h = pl.program_id(1)
    rows = q_ref.shape[0]
    scale = HEAD_DIM ** -0.5
    qs = _stack_groups(q_ref[...])
    s = _dot_nt(qs, k_ref[...]) * scale
    sink = _sink_column(sink_ref, h, rows)
    m = jnp.maximum(s.max(-1, keepdims=True), sink)
    p = jnp.exp(s - m)
    denom = p.sum(-1, keepdims=True) + jnp.exp(sink - m)
    o = jnp.dot((p / denom).astype(BF16), v_ref[...], preferred_element_type=F32)
    o_ref[...] = _unstack_groups(o, rows).astype(o_ref.dtype)


def _ctx_attention(qkv_c, sink):
    b, c, _ = qkv_c.shape
    qw = GQA_GROUP * HEAD_DIM
    k0 = N_Q_HEADS
    v0 = N_Q_HEADS + N_KV_HEADS
    return pl.pallas_call(
        _ctx_attn_body,
        out_shape=jax.ShapeDtypeStruct((b, c, N_Q_HEADS * HEAD_DIM), BF16),
        grid=(b, N_KV_HEADS),
        in_specs=[
            pl.BlockSpec(memory_space=pltpu.SMEM),
            pl.BlockSpec((None, c, qw), lambda bb, h: (bb, 0, h)),
            pl.BlockSpec((None, c, HEAD_DIM), lambda bb, h: (bb, 0, k0 + h)),
            pl.BlockSpec((None, c, HEAD_DIM), lambda bb, h: (bb, 0, v0 + h)),
        ],
        out_specs=pl.BlockSpec((None, c, qw), lambda bb, h: (bb, 0, h)),
        compiler_params=_params(("arbitrary", "arbitrary"), 16 * 1024 * 1024),
        name="context_attention",
    )(sink, qkv_c, qkv_c, qkv_c)


def _block_diag(ub, w_ref, bias):
    bw = w_ref.shape[1]
    return jnp.concatenate(
        [jnp.dot(ub[:, n * bw:(n + 1) * bw], w_ref[n], preferred_element_type=F32)
         for n in range(w_ref.shape[0])], axis=1) + bias


def _lru_body(up_ref, uc_ref, un_ref, cw_ref, cb_ref, wa_ref, ba_ref, wx_ref, bx_ref, lam_ref,
              h0_ref, y_ref, hl_ref, h_scr, a_scr, b_scr, *, nt, reverse):
    i = pl.program_id(1)
    ti = (nt - 1 - i) if reverse else i
    tm = uc_ref.shape[0]
    halo = up_ref.shape[0]

    @pl.when(i == 0)
    def _():
        h_scr[...] = h0_ref[...]

    prev = jnp.where(ti > 0, up_ref[...], 0.0)
    nxt = jnp.where(ti < nt - 1, un_ref[...], 0.0)
    ext = jnp.concatenate([prev, uc_ref[...], nxt], axis=0)
    u = cb_ref[...]
    for k in range(CONV_W):
        off = halo - CONV_LEFT + k
        u = u + ext[off:off + tm] * cw_ref[k:k + 1, :]

    ub = u.astype(BF16)
    r = jax.nn.sigmoid(_block_diag(ub, wa_ref, ba_ref[...]))
    ig = jax.nn.sigmoid(_block_diag(ub, wx_ref, bx_ref[...]))
    nl = -lam_ref[...]
    softplus = jnp.maximum(nl, 0.0) + jnp.log1p(jnp.exp(-jnp.abs(nl)))
    log_a = -LRU_C * r * softplus
    a_scr[...] = jnp.exp(log_a)
    b_scr[...] = jnp.sqrt(1.0 - jnp.exp(2.0 * log_a)) * ig * u

    ng = tm // SUBLANES
    row = lax.broadcasted_iota(jnp.int32, (SUBLANES, a_scr.shape[1]), 0)

    def group(g, h):
        gi = (ng - 1 - g) if reverse else g
        r0 = pl.multiple_of(gi * SUBLANES, SUBLANES)
        a = a_scr[pl.ds(r0, SUBLANES), :]
        bb = b_scr[pl.ds(r0, SUBLANES), :]
        for k in (1, 2, 4):
            if reverse:
                keep = row < SUBLANES - k
                shift = SUBLANES - k
            else:
                keep = row >= k
                shift = k
            a_sh = pltpu.roll(a, shift, 0)
            b_sh = pltpu.roll(bb, shift, 0)
            bb = bb + a * jnp.where(keep, b_sh, 0.0)
            a = a * jnp.where(keep, a_sh, 1.0)
        y = bb + a * h
        y_ref[pl.ds(r0, SUBLANES), :] = y
        return y[0:1, :] if reverse else y[SUBLANES - 1:SUBLANES, :]

    h_last = lax.fori_loop(0, ng, group, h_scr[...])
    h_scr[...] = h_last

    @pl.when(i == nt - 1)
    def _():
        hl_ref[...] = h_last


def _lru_scan(gu, conv_w, conv_b, w_a, b_a, w_x, b_x, lam, h0, reverse):
    b, t, d2 = gu.shape
    d = d2 // 2
    tm = _tile(t, 256)
    nt = t // tm
    halo = SUBLANES
    hb = tm // halo
    nh = t // halo

    def tmap(i):
        return (nt - 1 - i) if reverse else i

    vec = lambda: pl.BlockSpec((1, d), lambda bb, i: (0, 0))
    in_specs = [
        pl.BlockSpec((None, halo, d), lambda bb, i: (bb, jnp.maximum(tmap(i) * hb - 1, 0), 1)),
        pl.BlockSpec((None, tm, d), lambda bb, i: (bb, tmap(i), 1)),
        pl.BlockSpec((None, halo, d), lambda bb, i: (bb, jnp.minimum((tmap(i) + 1) * hb, nh - 1), 1)),
        pl.BlockSpec((CONV_W, d), lambda bb, i: (0, 0)),
        vec(),
        pl.BlockSpec(w_a.shape, lambda bb, i: (0, 0, 0)),
        vec(),
        pl.BlockSpec(w_x.shape, lambda bb, i: (0, 0, 0)),
        vec(),
        vec(),
        pl.BlockSpec((None, 1, d), lambda bb, i: (bb, 0, 0)),
    ]
    vm = 2 * (tm + 2 * halo) * d * 4 + 2 * tm * d * 4 + 2 * tm * d * 4 + 8 * tm * d * 4 + 4 * w_a.size * 2
    return pl.pallas_call(
        functools.partial(_lru_body, nt=nt, reverse=reverse),
        out_shape=(jax.ShapeDtypeStruct((b, t, d), F32), jax.ShapeDtypeStruct((b, 1, d), F32)),
        grid=(b, nt),
        in_specs=in_specs,
        out_specs=(pl.BlockSpec((None, tm, d), lambda bb, i: (bb, tmap(i), 0)),
                   pl.BlockSpec((None, 1, d), lambda bb, i: (bb, 0, 0))),
        scratch_shapes=[pltpu.VMEM((1, d), F32), pltpu.VMEM((tm, d), F32), pltpu.VMEM((tm, d), F32)],
        compiler_params=_params(("arbitrary", "arbitrary"), vm),
        name="rglru_bwd" if reverse else "rglru_fwd",
    )(gu, gu, gu, conv_w, conv_b.reshape(1, d), w_a, b_a.reshape(1, d), w_x, b_x.reshape(1, d),
      lam.reshape(1, d), h0)


def _peer_query_body(x_ref, g_ref, sh_ref, sc_ref, wq_ref, keys_ref, ht_ref, st_ref):
    h = _norm_mod(x_ref[...], g_ref[...], sh_ref[...], sc_ref[...])
    hb = h.astype(BF16)
    ht = h.T.astype(BF16)
    for p in range(ht_ref.shape[0]):
        ht_ref[p] = ht[:, p * PEER_PIECE:(p + 1) * PEER_PIECE]
    q = jnp.dot(hb, wq_ref[...], preferred_element_type=F32).astype(BF16)
    for hp in range(2 * PEER_HEADS):
        qc = q[:, hp * PEER_KEY_DIM:(hp + 1) * PEER_KEY_DIM]
        st_ref[hp] = _dot_nt(keys_ref[hp % 2], qc)


def _peer_query(x, gain, shift, scale, wq, keys):
    b, t, d = x.shape
    tm = _tile(t, PEER_SUB)
    nt = t // tm
    nq = wq.shape[1]
    vm = 2 * tm * d * 4 + 2 * d * nq * 2 + 2 * d * tm * 2 + 2 * 2 * PEER_HEADS * N_KEYS * tm * 4 + 4 * tm * d * 4
    return pl.pallas_call(
        _peer_query_body,
        out_shape=(jax.ShapeDtypeStruct((b * nt, tm // PEER_PIECE, d, PEER_PIECE), BF16),
                   jax.ShapeDtypeStruct((2 * PEER_HEADS, N_KEYS, b * t), F32)),
        grid=(b, nt),
        in_specs=[
            pl.BlockSpec((None, tm, d), lambda bb, i: (bb, i, 0)),
            pl.BlockSpec((1, d), lambda bb, i: (0, 0)),
            pl.BlockSpec((None, 1, d), lambda bb, i: (bb, 0, 0)),
            pl.BlockSpec((None, 1, d), lambda bb, i: (bb, 0, 0)),
            pl.BlockSpec((d, nq), lambda bb, i: (0, 0)),
            pl.BlockSpec(keys.shape, lambda bb, i: (0, 0, 0)),
        ],
        out_specs=(pl.BlockSpec((None, tm // PEER_PIECE, d, PEER_PIECE),
                                lambda bb, i: (bb * nt + i, 0, 0, 0)),
                   pl.BlockSpec((2 * PEER_HEADS, N_KEYS, tm), lambda bb, i: (0, 0, bb * nt + i))),
        compiler_params=_params(("arbitrary", "arbitrary"), vm),
        name="peer_query",
    )(x, gain.reshape(1, d), shift, scale, wq, keys)


def _oddeven_merge(lo, hi, r):
    step = r * 2
    if step < hi - lo:
        yield from _oddeven_merge(lo, hi, step)
        yield from _oddeven_merge(lo + r, hi, step)
        yield from [(k, k + r) for k in range(lo + r, hi - r, step)]
    else:
        yield (lo, lo + r)


def _oddeven_sort(lo, hi):
    if hi - lo >= 1:
        mid = lo + (hi - lo) // 2
        yield from _oddeven_sort(lo, mid)
        yield from _oddeven_sort(mid + 1, hi)
        yield from _oddeven_merge(lo, hi, 1)


_SORT16 = tuple(_oddeven_sort(0, PEER_TOPK - 1))


def _exchange(x, p, q):
    hi = jnp.maximum(x[p], x[q])
    lo = jnp.minimum(x[p], x[q])
    x[p] = hi
    x[q] = lo


def _merge_sublanes(x):
    n = len(x)
    shift = SUBLANES // 2
    while shift >= 1:
        z = [jnp.maximum(x[k], pltpu.roll(x[n - 1 - k], shift, 0)) for k in range(n)]
        dist = n // 2
        while dist >= 1:
            for k in range(n):
                if k & dist == 0:
                    _exchange(z, k, k + dist)
            dist //= 2
        x = z
        shift //= 2
    return x


def _top16_sorted(s):
    x = [s[SUBLANES * v:SUBLANES * (v + 1)] for v in range(s.shape[0] // SUBLANES)]
    assert len(x) == PEER_TOPK
    for p, q in _SORT16:
        _exchange(x, p, q)
    return _merge_sublanes(x)


def _dup16(v):
    bits = pltpu.bitcast(v.astype(BF16).astype(F32), jnp.uint32) >> 16
    return bits | (bits << 16)


def _route_stats(st_ref, e1d_scr, c1d_scr, e2_scr, r2_scr):
    tm = st_ref.shape[2]
    row = lax.broadcasted_iota(jnp.int32, (SUBLANES, tm), 0)
    for h in range(PEER_HEADS):
        s1 = st_ref[2 * h]
        s2 = st_ref[2 * h + 1]
        a = _top16_sorted(s1)
        b = _top16_sorted(s2)
        a_lo = a[SUBLANES - 1]
        a_hi = a[2 * SUBLANES - 1]
        for i in range(SUBLANES - 2, -1, -1):
            a_lo = jnp.where(row == i, a[i], a_lo)
            a_hi = jnp.where(row == i, a[SUBLANES + i], a_hi)
        c = [a_lo + b[j] for j in range(PEER_TOPK)]
        d = a_hi + b[0]
        c = [jnp.maximum(c[0], d)] + [
            jnp.maximum(c[j], jnp.minimum(c[j - 1], d)) for j in range(1, PEER_TOPK)]
        t = _merge_sublanes(c)
        z = jnp.ones_like(t[0])
        for k in range(1, PEER_TOPK):
            z = z + jnp.exp(t[k] - t[0])
        inv_z = 1.0 / z
        tau = t[PEER_TOPK - 1][0:1]
        count1 = jnp.zeros_like(s1)
        rank2 = jnp.zeros_like(s2)
        for j in range(PEER_TOPK):
            bj = b[j][0:1]
            count1 = count1 + jnp.where(s1 + bj >= tau, 1.0, 0.0)
            rank2 = rank2 + jnp.where(bj > s2, 1.0, 0.0)
        e1 = jnp.exp(s1 - a[0][0:1]) * inv_z[0:1]
        e2 = jnp.exp(s2 - b[0][0:1])
        for cc in range(tm // LANES):
            cs = slice(cc * LANES, (cc + 1) * LANES)
            c1d_scr[h, cc] = _dup16(count1[:, cs])
            e1d_scr[h, cc] = _dup16(e1[:, cs])
            r2_scr[h, cc] = rank2[:, cs].astype(BF16)
            e2_scr[h, cc] = e2[:, cs].astype(BF16)


def _route_body(st_ref, e1d_ref, c1d_ref, e2_ref, r2_ref):
    _route_stats(st_ref, e1d_ref, c1d_ref, e2_ref, r2_ref)


def _peer_route(st):
    t = st.shape[2]
    tm = _tile(t, PEER_SUB)
    nc = tm // LANES
    shape = (PEER_HEADS, t // LANES, N_KEYS, LANES)
    spec = pl.BlockSpec((PEER_HEADS, nc, N_KEYS, LANES), lambda i: (0, i, 0, 0))
    stat = PEER_HEADS * N_KEYS * tm
    return pl.pallas_call(
        _route_body,
        out_shape=(jax.ShapeDtypeStruct(shape, jnp.uint32), jax.ShapeDtypeStruct(shape, jnp.uint32),
                   jax.ShapeDtypeStruct(shape, BF16), jax.ShapeDtypeStruct(shape, BF16)),
        grid=(t // tm,),
        in_specs=[pl.BlockSpec((2 * PEER_HEADS, N_KEYS, tm), lambda i: (0, 0, i))],
        out_specs=(spec, spec, spec, spec),
        compiler_params=_params(("arbitrary",), 2 * 2 * stat * 4 + 2 * stat * 12 + 8 * stat),
        name="peer_route",
    )(st)


def _dense_act(zt_ref, act_ref, blk, c0, cols, e1d_ref, c1d_ref, e2_ref, r2_ref):
    eb = zt_ref.shape[1]
    for l in range(eb // N_KEYS):
        i1 = jnp.clip(blk * (eb // N_KEYS) + l, 0, N_KEYS - 1)
        rs = slice(l * N_KEYS, (l + 1) * N_KEYS)
        for c in cols:
            g = None
            for h in range(PEER_HEADS):
                cnt = jnp.broadcast_to(c1d_ref[h, c0 + c, pl.ds(i1, 1), :], (N_KEYS // 2, LANES))
                e1 = jnp.broadcast_to(e1d_ref[h, c0 + c, pl.ds(i1, 1), :], (N_KEYS // 2, LANES))
                gh = jnp.where(r2_ref[h, c0 + c] < pltpu.bitcast(cnt, BF16),
                               e2_ref[h, c0 + c] * pltpu.bitcast(e1, BF16), jnp.zeros((), BF16))
                g = gh if g is None else g + gh
            act_ref[c, rs, :] = jax.nn.gelu(zt_ref[c, rs, :].astype(BF16)) * g


def _peer_expert_body(ht_ref, e1d_hbm, c1d_hbm, e2_hbm, r2_hbm, u_ref, vt_ref, o_ref,
                      e1d_scr, c1d_scr, e2_scr, r2_scr, gate_sem, zt_scr, act_scr, *, eb):
    j = pl.program_id(1)
    ns = pl.num_programs(1) - 1
    gates = (e1d_scr, c1d_scr, e2_scr, r2_scr)
    n_sub, n_pc, _, pw = ht_ref.shape
    cpp = pw // LANES
    nc = n_pc * cpp

    def cols_of(p):
        return range(p * cpp, (p + 1) * cpp)

    def stage_a(k, s, p):
        z = jnp.dot(u_ref[s * eb:(s + 1) * eb, :], ht_ref[k, p], preferred_element_type=F32)
        for n, c in enumerate(cols_of(p)):
            zt_scr[k, s, c] = z[:, n * LANES:(n + 1) * LANES]

    def stage_b(k, s, p):
        _dense_act(zt_scr.at[k, 1 - s], act_scr.at[k, 1 - s], 2 * j - 1 + s, k * nc, cols_of(p),
                   *gates)

    def stage_c(k, s, p):
        act = jnp.concatenate([act_scr[k, s, c] for c in cols_of(p)], axis=1)
        o_ref[k, p] += jnp.dot(vt_ref[s], act, preferred_element_type=F32)

    def first_step(k, carry):
        for p in range(n_pc):
            stage_a(k, 0, p)
        for p in range(n_pc):
            stage_b(k, 1, p)
            stage_a(k, 1, p)
        return carry

    def middle_step(k, carry):
        for s in (0, 1):
            for p in range(n_pc):
                stage_c(k, s, p)
                stage_b(k, s, p)
                stage_a(k, s, p)
        return carry

    def last_step(k, carry):
        for p in range(n_pc):
            stage_c(k, 0, p)
            stage_b(k, 0, p)
        for p in range(n_pc):
            stage_c(k, 1, p)
        return carry

    @pl.when(j == 0)
    def _():
        gnc = e1d_scr.shape[1]
        first = pl.multiple_of(pl.program_id(0) * gnc, gnc)
        copies = [
            pltpu.make_async_copy(src.at[:, pl.ds(first, gnc)], dst, gate_sem.at[n])
            for n, (src, dst) in enumerate(zip((e1d_hbm, c1d_hbm, e2_hbm, r2_hbm), gates))]
        for cp in copies:
            cp.start()
        o_ref[...] = jnp.zeros_like(o_ref)
        for cp in copies:
            cp.wait()
        lax.fori_loop(0, n_sub, first_step, 0)

    @pl.when((j > 0) & (j < ns))
    def _():
        lax.fori_loop(0, n_sub, middle_step, 0)

    @pl.when(j == ns)
    def _():
        lax.fori_loop(0, n_sub, last_step, 0)


def _peer_experts(ht, gates, u, vt):
    n_tiles, n_pc, d, pw = ht.shape
    tm = n_pc * pw
    e = u.shape[0]
    g = 2 if n_tiles % 2 == 0 else 1
    nc = tm // LANES
    eb = PEER_EB
    ns = e // (2 * eb)
    assert vt.shape == (ns, 2, d, eb)
    once = pl.Buffered(1)
    gate_spec = pl.BlockSpec(memory_space=pl.ANY)
    gate_shape = (PEER_HEADS, g * nc, N_KEYS, LANES)
    in_specs = [
        pl.BlockSpec((g, n_pc, d, pw), lambda i, j: (i, 0, 0, 0), pipeline_mode=once),
        gate_spec, gate_spec, gate_spec, gate_spec,
        pl.BlockSpec((2 * eb, d), lambda i, j: (jnp.minimum(j, ns - 1), 0)),
        pl.BlockSpec((None, 2, d, eb), lambda i, j: (jnp.maximum(j - 1, 0), 0, 0, 0)),
    ]
    stat = PEER_HEADS * N_KEYS * tm
    vm = g * (d * tm * 2 + stat * 12 + 2 * d * tm * 4 + 2 * eb * tm * 6) + 8 * eb * d * 2
    return pl.pallas_call(
        functools.partial(_peer_expert_body, eb=eb),
        out_shape=jax.ShapeDtypeStruct((n_tiles, n_pc, d, pw), F32),
        grid=(n_tiles // g, ns + 1),
        in_specs=in_specs,
        out_specs=pl.BlockSpec((g, n_pc, d, pw), lambda i, j: (i, 0, 0, 0)),
        scratch_shapes=[
            pltpu.VMEM(gate_shape, jnp.uint32),
            pltpu.VMEM(gate_shape, jnp.uint32),
            pltpu.VMEM(gate_shape, BF16),
            pltpu.VMEM(gate_shape, BF16),
            pltpu.SemaphoreType.DMA((4,)),
            pltpu.VMEM((g, 2, nc, eb, LANES), F32),
            pltpu.VMEM((g, 2, nc, eb, LANES), BF16),
        ],
        compiler_params=_params(("arbitrary", "arbitrary"), vm),
        name="peer_experts",
    )(ht, *gates, u, vt)


def _peer_residual_body(*refs, final_norm):
    if final_norm:
        xs_ref, gate_ref, ft_ref, gain_ref, o_ref = refs
    else:
        xs_ref, gate_ref, ft_ref, o_ref = refs
    f = jnp.concatenate([ft_ref[p].T for p in range(ft_ref.shape[0])], axis=0)
    y = xs_ref[...] + gate_ref[...] * f
    if final_norm:
        ms = jnp.mean(y * y, axis=-1, keepdims=True)
        y = (y * lax.rsqrt(ms + NORM_EPS)) * gain_ref[...]
    o_ref[...] = y


def _peer_residual(xs, gate, ft, final_gain=None):
    b, t, d = xs.shape
    n_pc, pw = ft.shape[1], ft.shape[3]
    tm = n_pc * pw
    nt = t // tm
    final_norm = final_gain is not None
    in_specs = [
        pl.BlockSpec((None, tm, d), lambda bb, i: (bb, i, 0)),
        pl.BlockSpec((None, 1, d), lambda bb, i: (bb, 0, 0)),
        pl.BlockSpec((None, n_pc, d, pw), lambda bb, i: (bb * nt + i, 0, 0, 0)),
    ]
    args = [xs, gate, ft]
    if final_norm:
        in_specs.append(pl.BlockSpec((1, d), lambda bb, i: (0, 0)))
        args.append(final_gain.reshape(1, d))
    return pl.pallas_call(
        functools.partial(_peer_residual_body, final_norm=final_norm),
        out_shape=jax.ShapeDtypeStruct((b, t, d), F32),
        grid=(b, nt),
        in_specs=in_specs,
        out_specs=pl.BlockSpec((None, tm, d), lambda bb, i: (bb, i, 0)),
        compiler_params=_params(("arbitrary", "arbitrary"), 8 * tm * d * 4),
        name="peer_residual",
    )(*args)


def _tables_body(u_ref, v_ref, ub_ref, vt_ref):
    ub_ref[...] = u_ref[...].astype(BF16)
    vt_ref[...] = v_ref[...].T.astype(BF16)


def _peer_tables(u, v, layer):
    _, e, d = u.shape
    eb = PEER_EB
    table = pl.BlockSpec((None, eb, d), lambda i: (layer, i, 0))
    return pl.pallas_call(
        _tables_body,
        out_shape=(jax.ShapeDtypeStruct((e, d), BF16),
                   jax.ShapeDtypeStruct((e // (2 * eb), 2, d, eb), BF16)),
        grid=(e // eb,),
        in_specs=[table, table],
        out_specs=(pl.BlockSpec((eb, d), lambda i: (i, 0)),
                   pl.BlockSpec((None, None, d, eb), lambda i: (i // 2, i % 2, 0, 0))),
        compiler_params=_params(("arbitrary",), 2 * 2 * eb * d * 4 + 2 * 2 * eb * d * 2 + 2 * eb * d * 4),
        name="peer_tables",
    )(u, v)


def _peer(xs, gain, shift, scale, gate, wq, keys, u, vt, final_gain=None):
    ht, st = _peer_query(xs, gain, shift, scale, wq, keys)
    ft = _peer_experts(ht, _peer_route(st), u, vt)
    return _peer_residual(xs, gate, ft, final_gain)


def _rope_tables(s):
    t = jnp.arange(s)
    row = (t // GRID_W).astype(F32)
    col = (t % GRID_W).astype(F32)
    half = HEAD_DIM // 2
    inv = ROPE_BASE ** (-jnp.arange(0, half, 2, dtype=F32) / half)
    ang_r = row[:, None] * inv[None, :]
    ang_c = col[:, None] * inv[None, :]
    ang = jnp.concatenate([ang_r, ang_r, ang_c, ang_c], axis=-1)
    lane = jnp.arange(HEAD_DIM)
    sign = jnp.where((lane % half) < half // 2, -1.0, 1.0).astype(F32)
    return jnp.cos(ang), jnp.sin(ang) * sign[None, :]


def kernel(x, c, ctx, c_ctx, w_mod, b_mod, norm_mix, norm_ffn, norm_final, attn_w_qkv, attn_w_o, attn_sink, rec_w_in, rec_conv_w, rec_conv_b, rec_w_a, rec_b_a, rec_w_x, rec_b_x, rec_lambda, rec_w_out, peer_w_q, peer_keys, peer_u, peer_v):
    b, s, d = x.shape
    cl = ctx.shape[1]
    depth = w_mod.shape[0]
    rows = -(-(b + 1) // SUBLANES) * SUBLANES
    cvec = jnp.zeros((rows, d), F32).at[:b].set(c).at[b].set(c_ctx)
    mod = _mod_vectors(cvec, w_mod, b_mod).reshape(depth, rows, N_MOD, d)
    cos, sin_signed = _rope_tables(s)
    q_cols = N_Q_HEADS * HEAD_DIM
    kv_cols = N_KV_HEADS * HEAD_DIM

    xs = x
    cs = ctx.reshape(1, b * cl, d)
    for i in range(depth):
        last = i == depth - 1
        j = i // 2
        mx = [mod[i, :b, k][:, None, :] for k in range(N_MOD)]
        mc = [mod[i, b:b + 1, k][:, None, :] for k in range(N_MOD)]

        if i % 2 == 0:
            wqkv = attn_w_qkv[j].astype(BF16)
            wo = attn_w_o[j].astype(BF16)
            qkv_x = _proj(xs, norm_mix[i], mx[0], mx[1], wqkv, BF16, 512,
                          rope=(cos, sin_signed, q_cols + kv_cols))
            qkv_c = _proj(cs, norm_mix[i], mc[0], mc[1], wqkv, BF16, 512).reshape(b, cl, -1)
            o_x = _attention(qkv_x, qkv_c, attn_sink[j])
            xs = _oproj(o_x, wo, xs, mx[2])
            if not last:
                o_c = _ctx_attention(qkv_c, attn_sink[j]).reshape(1, b * cl, q_cols)
                cs = _oproj(o_c, wo, cs, mc[2])
        else:
            w_in = rec_w_in[j].astype(BF16)
            w_out = rec_w_out[j].astype(BF16)
            w_a = rec_w_a[j].astype(BF16)
            w_x = rec_w_x[j].astype(BF16)
            gu_x = _proj(xs, norm_mix[i], mx[0], mx[1], w_in, F32, 1024)
            gu_c = _proj(cs, norm_mix[i], mc[0], mc[1], w_in, F32, 1024).reshape(b, cl, -1)
            h0 = jnp.zeros((b, 1, d), F32)
            ys_c, ys_x = [], []
            for r in range(2):
                lru = functools.partial(
                    _lru_scan, conv_w=rec_conv_w[j], conv_b=rec_conv_b[j], w_a=w_a[r],
                    b_a=rec_b_a[j, r], w_x=w_x[r], b_x=rec_b_x[j, r], lam=rec_lambda[j, r],
                    reverse=(r == 1))
                y_c, h_c = lru(gu_c, h0=h0)
                y_x, _ = lru(gu_x, h0=h_c)
                ys_c.append(y_c)
                ys_x.append(y_x)
            xs = _rec_oproj(ys_x[0], ys_x[1], gu_x, w_out, xs, mx[2])
            if not last:
                cs = _rec_oproj(ys_c[0].reshape(1, b * cl, d), ys_c[1].reshape(1, b * cl, d),
                                gu_c.reshape(1, b * cl, -1), w_out, cs, mc[2])

        wq = peer_w_q[i].astype(BF16)
        keys = peer_keys[i].astype(BF16)
        u, vt = _peer_tables(peer_u, peer_v, i)
        xs = _peer(xs, norm_ffn[i], mx[3], mx[4], mx[5], wq, keys, u, vt,
                   final_gain=norm_final if last else None)
        if not last:
            cs = _peer(cs, norm_ffn[i], mc[3], mc[4], mc[5], wq, keys, u, vt)
    return xs
```

```python
import functools
import math

import jax
import jax.numpy as jnp
from jax import lax
from jax.experimental import pallas as pl
from jax.experimental.pallas import tpu as pltpu

F32 = jnp.float32
BF16 = jnp.bfloat16

NORM_EPS = 1e-6
N_MOD = 6
GRID_W = 64
HEAD_DIM = 128
N_Q_HEADS = 16
N_KV_HEADS = 4
GQA_GROUP = N_Q_HEADS // N_KV_HEADS
ATTN_BLOCK = 128
ROPE_BASE = 10000.0
RNN_BLOCKS = 8
CONV_W = 4
CONV_LEFT = 1
LRU_C = 8.0
PEER_HEADS = 8
PEER_KEY_DIM = 128
N_KEYS = 128
PEER_TOPK = 16
PEER_EB = 512
PEER_SUB = 512
PEER_PIECE = 512

V7X_VMEM_BYTES = 64 * 1024 * 1024
SUBLANES = 8
LANES = 128
VMEM_CAP = V7X_VMEM_BYTES - 6 * 1024 * 1024


def _params(sem, vmem_bytes):
    limit = int(min(VMEM_CAP, max(32 * 1024 * 1024, vmem_bytes * 5 // 4)))
    return pltpu.CompilerParams(dimension_semantics=sem, vmem_limit_bytes=limit)


def _tile(n, pref):
    return pref if n % pref == 0 else n


def _norm_mod(x, gain, shift, scale):
    ms = jnp.mean(x * x, axis=-1, keepdims=True)
    y = x * lax.rsqrt(ms + NORM_EPS)
    return (y * gain) * (1.0 + scale) + shift


def _dot_nt(a, b):
    return lax.dot_general(a, b, (((1,), (1,)), ((), ())), preferred_element_type=F32)


def _mod_body(c_ref, w_ref, b_ref, o_ref):
    s = jax.nn.silu(c_ref[...])
    o_ref[0] = jnp.dot(s.astype(BF16), w_ref[0].astype(BF16), preferred_element_type=F32) + b_ref[0]


def _mod_vectors(cvec, w_mod, b_mod):
    depth, d, n = w_mod.shape
    r = cvec.shape[0]
    tn = _tile(n, 1536)
    vm = 2 * d * tn * 4 + d * tn * 2 + 4 * r * tn * 4
    return pl.pallas_call(
        _mod_body,
        out_shape=jax.ShapeDtypeStruct((depth, r, n), F32),
        grid=(depth, n // tn),
        in_specs=[
            pl.BlockSpec((r, d), lambda i, j: (0, 0)),
            pl.BlockSpec((1, d, tn), lambda i, j: (i, 0, j)),
            pl.BlockSpec((1, 1, tn), lambda i, j: (i, 0, j)),
        ],
        out_specs=pl.BlockSpec((1, r, tn), lambda i, j: (i, 0, j)),
        compiler_params=_params(("arbitrary", "arbitrary"), vm),
        name="mod_vectors",
    )(cvec, w_mod, b_mod.reshape(depth, 1, n))


def _rope(x, cos, sin_signed):
    lane = lax.broadcasted_iota(jnp.int32, x.shape, 1)
    qtr = HEAD_DIM // 4
    first = (lane % (2 * qtr)) < qtr
    rot = jnp.where(first, pltpu.roll(x, HEAD_DIM - qtr, 1), pltpu.roll(x, qtr, 1))
    return x * cos + rot * sin_signed


def _proj_body(*refs, tn, rope_cols):
    if rope_cols:
        x_ref, g_ref, sh_ref, sc_ref, w_ref, cos_ref, sin_ref, o_ref, h_scr = refs
    else:
        x_ref, g_ref, sh_ref, sc_ref, w_ref, o_ref, h_scr = refs
    h_scr[...] = _norm_mod(x_ref[...], g_ref[...], sh_ref[...], sc_ref[...]).astype(BF16)
    for j in range(w_ref.shape[1] // tn):
        ns = slice(j * tn, (j + 1) * tn)
        y = jnp.dot(h_scr[...], w_ref[:, ns], preferred_element_type=F32)
        if j * tn < rope_cols:
            cos = cos_ref[...]
            sin = sin_ref[...]
            y = jnp.concatenate(
                [_rope(y[:, k * HEAD_DIM:(k + 1) * HEAD_DIM], cos, sin)
                 for k in range(tn // HEAD_DIM)], axis=1)
        o_ref[:, ns] = y.astype(o_ref.dtype)


def _proj(x, gain, shift, scale, w, out_dtype, tn, rope=None):
    b, t, d = x.shape
    n = w.shape[1]
    tm = _tile(t, 512)
    in_specs = [
        pl.BlockSpec((None, tm, d), lambda bb, i: (bb, i, 0)),
        pl.BlockSpec((1, d), lambda bb, i: (0, 0)),
        pl.BlockSpec((None, 1, d), lambda bb, i: (bb, 0, 0)),
        pl.BlockSpec((None, 1, d), lambda bb, i: (bb, 0, 0)),
        pl.BlockSpec((d, n), lambda bb, i: (0, 0), pipeline_mode=pl.Buffered(1)),
    ]
    args = [x, gain.reshape(1, d), shift, scale, w]
    rope_cols = 0
    if rope is not None:
        cos, sin_signed, rope_cols = rope
        assert rope_cols % tn == 0
        in_specs += [pl.BlockSpec((tm, HEAD_DIM), lambda bb, i: (i, 0))] * 2
        args += [cos, sin_signed]
    osz = jnp.dtype(out_dtype).itemsize
    vm = 2 * tm * d * 4 + d * n * 2 + 2 * tm * n * osz + tm * d * 2 + 3 * tm * tn * 4
    return pl.pallas_call(
        functools.partial(_proj_body, tn=tn, rope_cols=rope_cols),
        out_shape=jax.ShapeDtypeStruct((b, t, n), out_dtype),
        grid=(b, t // tm),
        in_specs=in_specs,
        out_specs=pl.BlockSpec((None, tm, n), lambda bb, i: (bb, i, 0)),
        scratch_shapes=[pltpu.VMEM((tm, d), BF16)],
        compiler_params=_params(("arbitrary", "arbitrary"), vm),
        name="norm_mod_proj",
    )(*args)


def _oproj_body(a_ref, w_ref, res_ref, gate_ref, o_ref):
    y = jnp.dot(a_ref[...], w_ref[...], preferred_element_type=F32)
    o_ref[...] = res_ref[...] + gate_ref[...] * y


def _rec_oproj_body(yf_ref, yb_ref, gu_ref, w_ref, res_ref, gate_ref, o_ref):
    a = (yf_ref[...] + yb_ref[...]) * jax.nn.gelu(gu_ref[...])
    y = jnp.dot(a.astype(BF16), w_ref[...], preferred_element_type=F32)
    o_ref[...] = res_ref[...] + gate_ref[...] * y


def _oproj(a, w, res, gate):
    b, t, k = a.shape
    d = w.shape[1]
    tm = _tile(t, 512)
    vm = 2 * tm * k * 2 + 2 * k * d * 2 + 4 * tm * d * 4 + tm * d * 4
    return pl.pallas_call(
        _oproj_body,
        out_shape=jax.ShapeDtypeStruct((b, t, d), F32),
        grid=(b, t // tm),
        in_specs=[
            pl.BlockSpec((None, tm, k), lambda bb, i: (bb, i, 0)),
            pl.BlockSpec((k, d), lambda bb, i: (0, 0)),
            pl.BlockSpec((None, tm, d), lambda bb, i: (bb, i, 0)),
            pl.BlockSpec((None, 1, d), lambda bb, i: (bb, 0, 0)),
        ],
        out_specs=pl.BlockSpec((None, tm, d), lambda bb, i: (bb, i, 0)),
        compiler_params=_params(("arbitrary", "arbitrary"), vm),
        name="oproj_residual",
    )(a, w, res, gate)


def _rec_oproj(yf, yb, gu, w, res, gate):
    b, t, k = yf.shape
    d = w.shape[1]
    tm = _tile(t, 256)
    vm = 6 * tm * k * 4 + 2 * k * d * 2 + 4 * tm * d * 4 + 3 * tm * d * 4
    return pl.pallas_call(
        _rec_oproj_body,
        out_shape=jax.ShapeDtypeStruct((b, t, d), F32),
        grid=(b, t // tm),
        in_specs=[
            pl.BlockSpec((None, tm, k), lambda bb, i: (bb, i, 0)),
            pl.BlockSpec((None, tm, k), lambda bb, i: (bb, i, 0)),
            pl.BlockSpec((None, tm, k), lambda bb, i: (bb, i, 0)),
            pl.BlockSpec((k, d), lambda bb, i: (0, 0)),
            pl.BlockSpec((None, tm, d), lambda bb, i: (bb, i, 0)),
            pl.BlockSpec((None, 1, d), lambda bb, i: (bb, 0, 0)),
        ],
        out_specs=pl.BlockSpec((None, tm, d), lambda bb, i: (bb, i, 0)),
        compiler_params=_params(("arbitrary", "arbitrary"), vm),
        name="rec_oproj_residual",
    )(yf, yb, gu, w, res, gate)


def _stack_groups(q):
    return jnp.concatenate(
        [q[:, g * HEAD_DIM:(g + 1) * HEAD_DIM] for g in range(GQA_GROUP)], axis=0)


def _unstack_groups(o, rows):
    return jnp.concatenate([o[g * rows:(g + 1) * rows] for g in range(GQA_GROUP)], axis=1)


def _sink_column(sink_ref, h, rows):
    return jnp.concatenate(
        [jnp.full((rows, 1), sink_ref[h * GQA_GROUP + g], F32) for g in range(GQA_GROUP)], axis=0)


def _attn_body(sink_ref, q_ref, kp_ref, kc_ref, kn_ref, vp_ref, vc_ref, vn_ref, kx_ref, vx_ref,
               o_ref, *, nb):
    i = pl.program_id(1)
    blk = ATTN_BLOCK
    scale = HEAD_DIM ** -0.5
    r = lax.broadcasted_iota(jnp.int32, (GQA_GROUP * blk, blk), 0) % blk
    c = lax.broadcasted_iota(jnp.int32, (GQA_GROUP * blk, blk), 1)
    keep_p = (c >= r) & (i >= 1)
    keep_n = (c <= r) & (i + 1 < nb)
    neg = -jnp.inf
    qw = GQA_GROUP * HEAD_DIM
    for h in range(N_KV_HEADS):
        hs = slice(h * HEAD_DIM, (h + 1) * HEAD_DIM)
        qs = _stack_groups(q_ref[:, h * qw:(h + 1) * qw])
        s_p = jnp.where(keep_p, _dot_nt(qs, kp_ref[:, hs]) * scale, neg)
        s_c = _dot_nt(qs, kc_ref[:, hs]) * scale
        s_n = jnp.where(keep_n, _dot_nt(qs, kn_ref[:, hs]) * scale, neg)
        s_x = _dot_nt(qs, kx_ref[:, hs]) * scale
        sink = _sink_column(sink_ref, h, blk)
        m = jnp.maximum(
            jnp.maximum(jnp.maximum(s_p.max(-1, keepdims=True), s_c.max(-1, keepdims=True)),
                        jnp.maximum(s_n.max(-1, keepdims=True), s_x.max(-1, keepdims=True))),
            sink)
        p_p = jnp.exp(s_p - m)
        p_c = jnp.exp(s_c - m)
        p_n = jnp.exp(s_n - m)
        p_x = jnp.exp(s_x - m)
        denom = (p_p.sum(-1, keepdims=True) + p_c.sum(-1, keepdims=True)
                 + p_n.sum(-1, keepdims=True) + p_x.sum(-1, keepdims=True) + jnp.exp(sink - m))
        o = (jnp.dot(p_p.astype(BF16), vp_ref[:, hs], preferred_element_type=F32)
             + jnp.dot(p_c.astype(BF16), vc_ref[:, hs], preferred_element_type=F32)
             + jnp.dot(p_n.astype(BF16), vn_ref[:, hs], preferred_element_type=F32)
             + jnp.dot(p_x.astype(BF16), vx_ref[:, hs], preferred_element_type=F32))
        o_ref[:, h * qw:(h + 1) * qw] = _unstack_groups(o / denom, blk).astype(o_ref.dtype)


def _attention(qkv_x, qkv_c, sink):
    b, s, _ = qkv_x.shape
    c = qkv_c.shape[1]
    nb = s // ATTN_BLOCK
    q_cols = N_Q_HEADS * HEAD_DIM
    kv_cols = N_KV_HEADS * HEAD_DIM
    k0 = q_cols // kv_cols
    blk = ATTN_BLOCK

    def kv_spec(col, off):
        return pl.BlockSpec((None, blk, kv_cols),
                            lambda bb, i: (bb, jnp.clip(i + off, 0, nb - 1), col))

    in_specs = [
        pl.BlockSpec(memory_space=pltpu.SMEM),
        pl.BlockSpec((None, blk, q_cols), lambda bb, i: (bb, i, 0)),
        kv_spec(k0, -1), kv_spec(k0, 0), kv_spec(k0, 1),
        kv_spec(k0 + 1, -1), kv_spec(k0 + 1, 0), kv_spec(k0 + 1, 1),
        pl.BlockSpec((None, c, kv_cols), lambda bb, i: (bb, 0, k0)),
        pl.BlockSpec((None, c, kv_cols), lambda bb, i: (bb, 0, k0 + 1)),
    ]
    return pl.pallas_call(
        functools.partial(_attn_body, nb=nb),
        out_shape=jax.ShapeDtypeStruct((b, s, q_cols), BF16),
        grid=(b, nb),
        in_specs=in_specs,
        out_specs=pl.BlockSpec((None, blk, q_cols), lambda bb, i: (bb, i, 0)),
        compiler_params=_params(("arbitrary", "arbitrary"), 24 * 1024 * 1024),
        name="window_attention",
    )(sink, qkv_x, qkv_x, qkv_x, qkv_x, qkv_x, qkv_x, qkv_x, qkv_c, qkv_c)


def _ctx_attn_body(sink_ref, q_ref, k_ref, v_ref, o_ref):
    h = pl.program_id(1)
    rows = q_ref.shape[0]
    scale = HEAD_DIM ** -0.5
    qs = _stack_groups(q_ref[...])
    s = _dot_nt(qs, k_ref[...]) * scale
    sink = _sink_column(sink_ref, h, rows)
    m = jnp.maximum(s.max(-1, keepdims=True), sink)
    p = jnp.exp(s - m)
    denom = p.sum(-1, keepdims=True) + jnp.exp(sink - m)
    o = jnp.dot((p / denom).astype(BF16), v_ref[...], preferred_element_type=F32)
    o_ref[...] = _unstack_groups(o, rows).astype(o_ref.dtype)


def _ctx_attention(qkv_c, sink):
    b, c, _ = qkv_c.shape
    qw = GQA_GROUP * HEAD_DIM
    k0 = N_Q_HEADS
    v0 = N_Q_HEADS + N_KV_HEADS
    return pl.pallas_call(
        _ctx_attn_body,
        out_shape=jax.ShapeDtypeStruct((b, c, N_Q_HEADS * HEAD_DIM), BF16),
        grid=(b, N_KV_HEADS),
        in_specs=[
            pl.BlockSpec(memory_space=pltpu.SMEM),
            pl.BlockSpec((None, c, qw), lambda bb, h: (bb, 0, h)),
            pl.BlockSpec((None, c, HEAD_DIM), lambda bb, h: (bb, 0, k0 + h)),
            pl.BlockSpec((None, c, HEAD_DIM), lambda bb, h: (bb, 0, v0 + h)),
        ],
        out_specs=pl.BlockSpec((None, c, qw), lambda bb, h: (bb, 0, h)),
        compiler_params=_params(("arbitrary", "arbitrary"), 16 * 1024 * 1024),
        name="context_attention",
    )(sink, qkv_c, qkv_c, qkv_c)


def _block_diag(ub, w_ref, bias):
    bw = w_ref.shape[1]
    return jnp.concatenate(
        [jnp.dot(ub[:, n * bw:(n + 1) * bw], w_ref[n], preferred_element_type=F32)
         for n in range(w_ref.shape[0])], axis=1) + bias


def _lru_body(up_ref, uc_ref, un_ref, cw_ref, cb_ref, wa_ref, ba_ref, wx_ref, bx_ref, lam_ref,
              h0_ref, y_ref, hl_ref, h_scr, a_scr, b_scr, *, nt, reverse):
    i = pl.program_id(1)
    ti = (nt - 1 - i) if reverse else i
    tm = uc_ref.shape[0]
    halo = up_ref.shape[0]

    @pl.when(i == 0)
    def _():
        h_scr[...] = h0_ref[...]

    prev = jnp.where(ti > 0, up_ref[...], 0.0)
    nxt = jnp.where(ti < nt - 1, un_ref[...], 0.0)
    ext = jnp.concatenate([prev, uc_ref[...], nxt], axis=0)
    u = cb_ref[...]
    for k in range(CONV_W):
        off = halo - CONV_LEFT + k
        u = u + ext[off:off + tm] * cw_ref[k:k + 1, :]

    ub = u.astype(BF16)
    r = jax.nn.sigmoid(_block_diag(ub, wa_ref, ba_ref[...]))
    ig = jax.nn.sigmoid(_block_diag(ub, wx_ref, bx_ref[...]))
    nl = -lam_ref[...]
    softplus = jnp.maximum(nl, 0.0) + jnp.log1p(jnp.exp(-jnp.abs(nl)))
    log_a = -LRU_C * r * softplus
    a_scr[...] = jnp.exp(log_a)
    b_scr[...] = jnp.sqrt(1.0 - jnp.exp(2.0 * log_a)) * ig * u

    ng = tm // SUBLANES
    row = lax.broadcasted_iota(jnp.int32, (SUBLANES, a_scr.shape[1]), 0)

    def group(g, h):
        gi = (ng - 1 - g) if reverse else g
        r0 = pl.multiple_of(gi * SUBLANES, SUBLANES)
        a = a_scr[pl.ds(r0, SUBLANES), :]
        bb = b_scr[pl.ds(r0, SUBLANES), :]
        for k in (1, 2, 4):
            if reverse:
                keep = row < SUBLANES - k
                shift = SUBLANES - k
            else:
                keep = row >= k
                shift = k
            a_sh = pltpu.roll(a, shift, 0)
            b_sh = pltpu.roll(bb, shift, 0)
            bb = bb + a * jnp.where(keep, b_sh, 0.0)
            a = a * jnp.where(keep, a_sh, 1.0)
        y = bb + a * h
        y_ref[pl.ds(r0, SUBLANES), :] = y
        return y[0:1, :] if reverse else y[SUBLANES - 1:SUBLANES, :]

    h_last = lax.fori_loop(0, ng, group, h_scr[...])
    h_scr[...] = h_last

    @pl.when(i == nt - 1)
    def _():
        hl_ref[...] = h_last


def _lru_scan(gu, conv_w, conv_b, w_a, b_a, w_x, b_x, lam, h0, reverse):
    b, t, d2 = gu.shape
    d = d2 // 2
    tm = _tile(t, 256)
    nt = t // tm
    halo = SUBLANES
    hb = tm // halo
    nh = t // halo

    def tmap(i):
        return (nt - 1 - i) if reverse else i

    vec = lambda: pl.BlockSpec((1, d), lambda bb, i: (0, 0))
    in_specs = [
        pl.BlockSpec((None, halo, d), lambda bb, i: (bb, jnp.maximum(tmap(i) * hb - 1, 0), 1)),
        pl.BlockSpec((None, tm, d), lambda bb, i: (bb, tmap(i), 1)),
        pl.BlockSpec((None, halo, d), lambda bb, i: (bb, jnp.minimum((tmap(i) + 1) * hb, nh - 1), 1)),
        pl.BlockSpec((CONV_W, d), lambda bb, i: (0, 0)),
        vec(),
        pl.BlockSpec(w_a.shape, lambda bb, i: (0, 0, 0)),
        vec(),
        pl.BlockSpec(w_x.shape, lambda bb, i: (0, 0, 0)),
        vec(),
        vec(),
        pl.BlockSpec((None, 1, d), lambda bb, i: (bb, 0, 0)),
    ]
    vm = 2 * (tm + 2 * halo) * d * 4 + 2 * tm * d * 4 + 2 * tm * d * 4 + 8 * tm * d * 4 + 4 * w_a.size * 2
    return pl.pallas_call(
        functools.partial(_lru_body, nt=nt, reverse=reverse),
        out_shape=(jax.ShapeDtypeStruct((b, t, d), F32), jax.ShapeDtypeStruct((b, 1, d), F32)),
        grid=(b, nt),
        in_specs=in_specs,
        out_specs=(pl.BlockSpec((None, tm, d), lambda bb, i: (bb, tmap(i), 0)),
                   pl.BlockSpec((None, 1, d), lambda bb, i: (bb, 0, 0))),
        scratch_shapes=[pltpu.VMEM((1, d), F32), pltpu.VMEM((tm, d), F32), pltpu.VMEM((tm, d), F32)],
        compiler_params=_params(("arbitrary", "arbitrary"), vm),
        name="rglru_bwd" if reverse else "rglru_fwd",
    )(gu, gu, gu, conv_w, conv_b.reshape(1, d), w_a, b_a.reshape(1, d), w_x, b_x.reshape(1, d),
      lam.reshape(1, d), h0)


def _peer_query_body(x_ref, g_ref, sh_ref, sc_ref, wq_ref, keys_ref, ht_ref, st_ref):
    h = _norm_mod(x_ref[...], g_ref[...], sh_ref[...], sc_ref[...])
    hb = h.astype(BF16)
    ht = h.T.astype(BF16)
    for p in range(ht_ref.shape[0]):
        ht_ref[p] = ht[:, p * PEER_PIECE:(p + 1) * PEER_PIECE]
    q = jnp.dot(hb, wq_ref[...], preferred_element_type=F32).astype(BF16)
    for hp in range(2 * PEER_HEADS):
        qc = q[:, hp * PEER_KEY_DIM:(hp + 1) * PEER_KEY_DIM]
        st_ref[hp] = _dot_nt(keys_ref[hp % 2], qc)


def _peer_query(x, gain, shift, scale, wq, keys):
    b, t, d = x.shape
    tm = _tile(t, PEER_SUB)
    nt = t // tm
    nq = wq.shape[1]
    vm = 2 * tm * d * 4 + 2 * d * nq * 2 + 2 * d * tm * 2 + 2 * 2 * PEER_HEADS * N_KEYS * tm * 4 + 4 * tm * d * 4
    return pl.pallas_call(
        _peer_query_body,
        out_shape=(jax.ShapeDtypeStruct((b * nt, tm // PEER_PIECE, d, PEER_PIECE), BF16),
                   jax.ShapeDtypeStruct((2 * PEER_HEADS, N_KEYS, b * t), F32)),
        grid=(b, nt),
        in_specs=[
            pl.BlockSpec((None, tm, d), lambda bb, i: (bb, i, 0)),
            pl.BlockSpec((1, d), lambda bb, i: (0, 0)),
            pl.BlockSpec((None, 1, d), lambda bb, i: (bb, 0, 0)),
            pl.BlockSpec((None, 1, d), lambda bb, i: (bb, 0, 0)),
            pl.BlockSpec((d, nq), lambda bb, i: (0, 0)),
            pl.BlockSpec(keys.shape, lambda bb, i: (0, 0, 0)),
        ],
        out_specs=(pl.BlockSpec((None, tm // PEER_PIECE, d, PEER_PIECE),
                                lambda bb, i: (bb * nt + i, 0, 0, 0)),
                   pl.BlockSpec((2 * PEER_HEADS, N_KEYS, tm), lambda bb, i: (0, 0, bb * nt + i))),
        compiler_params=_params(("arbitrary", "arbitrary"), vm),
        name="peer_query",
    )(x, gain.reshape(1, d), shift, scale, wq, keys)


def _oddeven_merge(lo, hi, r):
    step = r * 2
    if step < hi - lo:
        yield from _oddeven_merge(lo, hi, step)
        yield from _oddeven_merge(lo + r, hi, step)
        yield from [(k, k + r) for k in range(lo + r, hi - r, step)]
    else:
        yield (lo, lo + r)


def _oddeven_sort(lo, hi):
    if hi - lo >= 1:
        mid = lo + (hi - lo) // 2
        yield from _oddeven_sort(lo, mid)
        yield from _oddeven_sort(mid + 1, hi)
        yield from _oddeven_merge(lo, hi, 1)


_SORT16 = tuple(_oddeven_sort(0, PEER_TOPK - 1))


def _exchange(x, p, q):
    hi = jnp.maximum(x[p], x[q])
    lo = jnp.minimum(x[p], x[q])
    x[p] = hi
    x[q] = lo


def _merge_sublanes(x):
    n = len(x)
    shift = SUBLANES // 2
    while shift >= 1:
        z = [jnp.maximum(x[k], pltpu.roll(x[n - 1 - k], shift, 0)) for k in range(n)]
        dist = n // 2
        while dist >= 1:
            for k in range(n):
                if k & dist == 0:
                    _exchange(z, k, k + dist)
            dist //= 2
        x = z
        shift //= 2
    return x


def _top16_sorted(s):
    x = [s[SUBLANES * v:SUBLANES * (v + 1)] for v in range(s.shape[0] // SUBLANES)]
    assert len(x) == PEER_TOPK
    for p, q in _SORT16:
        _exchange(x, p, q)
    return _merge_sublanes(x)


def _dup16(v):
    bits = pltpu.bitcast(v.astype(BF16).astype(F32), jnp.uint32) >> 16
    return bits | (bits << 16)


def _route_stats(st_ref, e1d_scr, c1d_scr, e2_scr, r2_scr):
    tm = st_ref.shape[2]
    row = lax.broadcasted_iota(jnp.int32, (SUBLANES, tm), 0)
    for h in range(PEER_HEADS):
        s1 = st_ref[2 * h]
        s2 = st_ref[2 * h + 1]
        a = _top16_sorted(s1)
        b = _top16_sorted(s2)
        a_lo = a[SUBLANES - 1]
        a_hi = a[2 * SUBLANES - 1]
        for i in range(SUBLANES - 2, -1, -1):
            a_lo = jnp.where(row == i, a[i], a_lo)
            a_hi = jnp.where(row == i, a[SUBLANES + i], a_hi)
        c = [a_lo + b[j] for j in range(PEER_TOPK)]
        d = a_hi + b[0]
        c = [jnp.maximum(c[0], d)] + [
            jnp.maximum(c[j], jnp.minimum(c[j - 1], d)) for j in range(1, PEER_TOPK)]
        t = _merge_sublanes(c)
        z = jnp.ones_like(t[0])
        for k in range(1, PEER_TOPK):
            z = z + jnp.exp(t[k] - t[0])
        inv_z = 1.0 / z
        tau = t[PEER_TOPK - 1][0:1]
        count1 = jnp.zeros_like(s1)
        rank2 = jnp.zeros_like(s2)
        for j in range(PEER_TOPK):
            bj = b[j][0:1]
            count1 = count1 + jnp.where(s1 + bj >= tau, 1.0, 0.0)
            rank2 = rank2 + jnp.where(bj > s2, 1.0, 0.0)
        e1 = jnp.exp(s1 - a[0][0:1]) * inv_z[0:1]
        e2 = jnp.exp(s2 - b[0][0:1])
        for cc in range(tm // LANES):
            cs = slice(cc * LANES, (cc + 1) * LANES)
            c1d_scr[h, cc] = _dup16(count1[:, cs])
            e1d_scr[h, cc] = _dup16(e1[:, cs])
            r2_scr[h, cc] = rank2[:, cs].astype(BF16)
            e2_scr[h, cc] = e2[:, cs].astype(BF16)


def _route_body(st_ref, e1d_ref, c1d_ref, e2_ref, r2_ref):
    _route_stats(st_ref, e1d_ref, c1d_ref, e2_ref, r2_ref)


def _peer_route(st):
    t = st.shape[2]
    tm = _tile(t, PEER_SUB)
    nc = tm // LANES
    shape = (PEER_HEADS, t // LANES, N_KEYS, LANES)
    spec = pl.BlockSpec((PEER_HEADS, nc, N_KEYS, LANES), lambda i: (0, i, 0, 0))
    stat = PEER_HEADS * N_KEYS * tm
    return pl.pallas_call(
        _route_body,
        out_shape=(jax.ShapeDtypeStruct(shape, jnp.uint32), jax.ShapeDtypeStruct(shape, jnp.uint32),
                   jax.ShapeDtypeStruct(shape, BF16), jax.ShapeDtypeStruct(shape, BF16)),
        grid=(t // tm,),
        in_specs=[pl.BlockSpec((2 * PEER_HEADS, N_KEYS, tm), lambda i: (0, 0, i))],
        out_specs=(spec, spec, spec, spec),
        compiler_params=_params(("arbitrary",), 2 * 2 * stat * 4 + 2 * stat * 12 + 8 * stat),
        name="peer_route",
    )(st)


def _dense_act(zt_ref, act_ref, blk, c0, cols, e1d_ref, c1d_ref, e2_ref, r2_ref):
    eb = zt_ref.shape[1]
    for l in range(eb // N_KEYS):
        i1 = jnp.clip(blk * (eb // N_KEYS) + l, 0, N_KEYS - 1)
        rs = slice(l * N_KEYS, (l + 1) * N_KEYS)
        for c in cols:
            g = None
            for h in range(PEER_HEADS):
                cnt = jnp.broadcast_to(c1d_ref[h, c0 + c, pl.ds(i1, 1), :], (N_KEYS // 2, LANES))
                e1 = jnp.broadcast_to(e1d_ref[h, c0 + c, pl.ds(i1, 1), :], (N_KEYS // 2, LANES))
                gh = jnp.where(r2_ref[h, c0 + c] < pltpu.bitcast(cnt, BF16),
                               e2_ref[h, c0 + c] * pltpu.bitcast(e1, BF16), jnp.zeros((), BF16))
                g = gh if g is None else g + gh
            act_ref[c, rs, :] = jax.nn.gelu(zt_ref[c, rs, :].astype(BF16)) * g


def _peer_expert_body(ht_ref, e1d_hbm, c1d_hbm, e2_hbm, r2_hbm, u_ref, vt_ref, o_ref,
                      e1d_scr, c1d_scr, e2_scr, r2_scr, gate_sem, zt_scr, act_scr, *, eb):
    j = pl.program_id(1)
    ns = pl.num_programs(1) - 1
    gates = (e1d_scr, c1d_scr, e2_scr, r2_scr)
    n_sub, n_pc, _, pw = ht_ref.shape
    cpp = pw // LANES
    nc = n_pc * cpp

    def cols_of(p):
        return range(p * cpp, (p + 1) * cpp)

    def stage_a(k, s, p):
        z = jnp.dot(u_ref[s * eb:(s + 1) * eb, :], ht_ref[k, p], preferred_element_type=F32)
        for n, c in enumerate(cols_of(p)):
            zt_scr[k, s, c] = z[:, n * LANES:(n + 1) * LANES]

    def stage_b(k, s, p):
        _dense_act(zt_scr.at[k, 1 - s], act_scr.at[k, 1 - s], 2 * j - 1 + s, k * nc, cols_of(p),
                   *gates)

    def stage_c(k, s, p):
        act = jnp.concatenate([act_scr[k, s, c] for c in cols_of(p)], axis=1)
        o_ref[k, p] += jnp.dot(vt_ref[s], act, preferred_element_type=F32)

    def first_step(k, carry):
        for p in range(n_pc):
            stage_a(k, 0, p)
        for p in range(n_pc):
            stage_b(k, 1, p)
            stage_a(k, 1, p)
        return carry

    def middle_step(k, carry):
        for s in (0, 1):
            for p in range(n_pc):
                stage_c(k, s, p)
                stage_b(k, s, p)
                stage_a(k, s, p)
        return carry

    def last_step(k, carry):
        for p in range(n_pc):
            stage_c(k, 0, p)
            stage_b(k, 0, p)
        for p in range(n_pc):
            stage_c(k, 1, p)
        return carry

    @pl.when(j == 0)
    def _():
        gnc = e1d_scr.shape[1]
        first = pl.multiple_of(pl.program_id(0) * gnc, gnc)
        copies = [
            pltpu.make_async_copy(src.at[:, pl.ds(first, gnc)], dst, gate_sem.at[n])
            for n, (src, dst) in enumerate(zip((e1d_hbm, c1d_hbm, e2_hbm, r2_hbm), gates))]
        for cp in copies:
            cp.start()
        o_ref[...] = jnp.zeros_like(o_ref)
        for cp in copies:
            cp.wait()
        lax.fori_loop(0, n_sub, first_step, 0)

    @pl.when((j > 0) & (j < ns))
    def _():
        lax.fori_loop(0, n_sub, middle_step, 0)

    @pl.when(j == ns)
    def _():
        lax.fori_loop(0, n_sub, last_step, 0)


def _peer_experts(ht, gates, u, vt):
    n_tiles, n_pc, d, pw = ht.shape
    tm = n_pc * pw
    e = u.shape[0]
    g = 2 if n_tiles % 2 == 0 else 1
    nc = tm // LANES
    eb = PEER_EB
    ns = e // (2 * eb)
    assert vt.shape == (ns, 2, d, eb)
    once = pl.Buffered(1)
    gate_spec = pl.BlockSpec(memory_space=pl.ANY)
    gate_shape = (PEER_HEADS, g * nc, N_KEYS, LANES)
    in_specs = [
        pl.BlockSpec((g, n_pc, d, pw), lambda i, j: (i, 0, 0, 0), pipeline_mode=once),
        gate_spec, gate_spec, gate_spec, gate_spec,
        pl.BlockSpec((2 * eb, d), lambda i, j: (jnp.minimum(j, ns - 1), 0)),
        pl.BlockSpec((None, 2, d, eb), lambda i, j: (jnp.maximum(j - 1, 0), 0, 0, 0)),
    ]
    stat = PEER_HEADS * N_KEYS * tm
    vm = g * (d * tm * 2 + stat * 12 + 2 * d * tm * 4 + 2 * eb * tm * 6) + 8 * eb * d * 2
    return pl.pallas_call(
        functools.partial(_peer_expert_body, eb=eb),
        out_shape=jax.ShapeDtypeStruct((n_tiles, n_pc, d, pw), F32),
        grid=(n_tiles // g, ns + 1),
        in_specs=in_specs,
        out_specs=pl.BlockSpec((g, n_pc, d, pw), lambda i, j: (i, 0, 0, 0)),
        scratch_shapes=[
            pltpu.VMEM(gate_shape, jnp.uint32),
            pltpu.VMEM(gate_shape, jnp.uint32),
            pltpu.VMEM(gate_shape, BF16),
            pltpu.VMEM(gate_shape, BF16),
            pltpu.SemaphoreType.DMA((4,)),
            pltpu.VMEM((g, 2, nc, eb, LANES), F32),
            pltpu.VMEM((g, 2, nc, eb, LANES), BF16),
        ],
        compiler_params=_params(("arbitrary", "arbitrary"), vm),
        name="peer_experts",
    )(ht, *gates, u, vt)


def _peer_residual_body(*refs, final_norm):
    if final_norm:
        xs_ref, gate_ref, ft_ref, gain_ref, o_ref = refs
    else:
        xs_ref, gate_ref, ft_ref, o_ref = refs
    f = jnp.concatenate([ft_ref[p].T for p in range(ft_ref.shape[0])], axis=0)
    y = xs_ref[...] + gate_ref[...] * f
    if final_norm:
        ms = jnp.mean(y * y, axis=-1, keepdims=True)
        y = (y * lax.rsqrt(ms + NORM_EPS)) * gain_ref[...]
    o_ref[...] = y


def _peer_residual(xs, gate, ft, final_gain=None):
    b, t, d = xs.shape
    n_pc, pw = ft.shape[1], ft.shape[3]
    tm = n_pc * pw
    nt = t // tm
    final_norm = final_gain is not None
    in_specs = [
        pl.BlockSpec((None, tm, d), lambda bb, i: (bb, i, 0)),
        pl.BlockSpec((None, 1, d), lambda bb, i: (bb, 0, 0)),
        pl.BlockSpec((None, n_pc, d, pw), lambda bb, i: (bb * nt + i, 0, 0, 0)),
    ]
    args = [xs, gate, ft]
    if final_norm:
        in_specs.append(pl.BlockSpec((1, d), lambda bb, i: (0, 0)))
        args.append(final_gain.reshape(1, d))
    return pl.pallas_call(
        functools.partial(_peer_residual_body, final_norm=final_norm),
        out_shape=jax.ShapeDtypeStruct((b, t, d), F32),
        grid=(b, nt),
        in_specs=in_specs,
        out_specs=pl.BlockSpec((None, tm, d), lambda bb, i: (bb, i, 0)),
        compiler_params=_params(("arbitrary", "arbitrary"), 8 * tm * d * 4),
        name="peer_residual",
    )(*args)


def _tables_body(u_ref, v_ref, ub_ref, vt_ref):
    ub_ref[...] = u_ref[...].astype(BF16)
    vt_ref[...] = v_ref[...].T.astype(BF16)


def _peer_tables(u, v, layer):
    _, e, d = u.shape
    eb = PEER_EB
    table = pl.BlockSpec((None, eb, d), lambda i: (layer, i, 0))
    return pl.pallas_call(
        _tables_body,
        out_shape=(jax.ShapeDtypeStruct((e, d), BF16),
                   jax.ShapeDtypeStruct((e // (2 * eb), 2, d, eb), BF16)),
        grid=(e // eb,),
        in_specs=[table, table],
        out_specs=(pl.BlockSpec((eb, d), lambda i: (i, 0)),
                   pl.BlockSpec((None, None, d, eb), lambda i: (i // 2, i % 2, 0, 0))),
        compiler_params=_params(("arbitrary",), 2 * 2 * eb * d * 4 + 2 * 2 * eb * d * 2 + 2 * eb * d * 4),
        name="peer_tables",
    )(u, v)


def _peer(xs, gain, shift, scale, gate, wq, keys, u, vt, final_gain=None):
    ht, st = _peer_query(xs, gain, shift, scale, wq, keys)
    ft = _peer_experts(ht, _peer_route(st), u, vt)
    return _peer_residual(xs, gate, ft, final_gain)


def _rope_tables(s):
    t = jnp.arange(s)
    row = (t // GRID_W).astype(F32)
    col = (t % GRID_W).astype(F32)
    half = HEAD_DIM // 2
    inv = ROPE_BASE ** (-jnp.arange(0, half, 2, dtype=F32) / half)
    ang_r = row[:, None] * inv[None, :]
    ang_c = col[:, None] * inv[None, :]
    ang = jnp.concatenate([ang_r, ang_r, ang_c, ang_c], axis=-1)
    lane = jnp.arange(HEAD_DIM)
    sign = jnp.where((lane % half) < half // 2, -1.0, 1.0).astype(F32)
    return jnp.cos(ang), jnp.sin(ang) * sign[None, :]


def kernel(x, c, ctx, c_ctx, w_mod, b_mod, norm_mix, norm_ffn, norm_final, attn_w_qkv, attn_w_o, attn_sink, rec_w_in, rec_conv_w, rec_conv_b, rec_w_a, rec_b_a, rec_w_x, rec_b_x, rec_lambda, rec_w_out, peer_w_q, peer_keys, peer_u, peer_v):
    b, s, d = x.shape
    cl = ctx.shape[1]
    depth = w_mod.shape[0]
    rows = -(-(b + 1) // SUBLANES) * SUBLANES
    cvec = jnp.zeros((rows, d), F32).at[:b].set(c).at[b].set(c_ctx)
    mod = _mod_vectors(cvec, w_mod, b_mod).reshape(depth, rows, N_MOD, d)
    cos, sin_signed = _rope_tables(s)
    q_cols = N_Q_HEADS * HEAD_DIM
    kv_cols = N_KV_HEADS * HEAD_DIM

    xs = x
    cs = ctx.reshape(1, b * cl, d)
    for i in range(depth):
        last = i == depth - 1
        j = i // 2
        mx = [mod[i, :b, k][:, None, :] for k in range(N_MOD)]
        mc = [mod[i, b:b + 1, k][:, None, :] for k in range(N_MOD)]

        if i % 2 == 0:
            wqkv = attn_w_qkv[j].astype(BF16)
            wo = attn_w_o[j].astype(BF16)
            qkv_x = _proj(xs, norm_mix[i], mx[0], mx[1], wqkv, BF16, 512,
                          rope=(cos, sin_signed, q_cols + kv_cols))
            qkv_c = _proj(cs, norm_mix[i], mc[0], mc[1], wqkv, BF16, 512).reshape(b, cl, -1)
            o_x = _attention(qkv_x, qkv_c, attn_sink[j])
            xs = _oproj(o_x, wo, xs, mx[2])
            if not last:
                o_c = _ctx_attention(qkv_c, attn_sink[j]).reshape(1, b * cl, q_cols)
                cs = _oproj(o_c, wo, cs, mc[2])
        else:
            w_in = rec_w_in[j].astype(BF16)
            w_out = rec_w_out[j].astype(BF16)
            w_a = rec_w_a[j].astype(BF16)
            w_x = rec_w_x[j].astype(BF16)
            gu_x = _proj(xs, norm_mix[i], mx[0], mx[1], w_in, F32, 1024)
            gu_c = _proj(cs, norm_mix[i], mc[0], mc[1], w_in, F32, 1024).reshape(b, cl, -1)
            h0 = jnp.zeros((b, 1, d), F32)
            ys_c, ys_x = [], []
            for r in range(2):
                lru = functools.partial(
                    _lru_scan, conv_w=rec_conv_w[j], conv_b=rec_conv_b[j], w_a=w_a[r],
                    b_a=rec_b_a[j, r], w_x=w_x[r], b_x=rec_b_x[j, r], lam=rec_lambda[j, r],
                    reverse=(r == 1))
                y_c, h_c = lru(gu_c, h0=h0)
                y_x, _ = lru(gu_x, h0=h_c)
                ys_c.append(y_c)
                ys_x.append(y_x)
            xs = _rec_oproj(ys_x[0], ys_x[1], gu_x, w_out, xs, mx[2])
            if not last:
                cs = _rec_oproj(ys_c[0].reshape(1, b * cl, d), ys_c[1].reshape(1, b * cl, d),
                                gu_c.reshape(1, b * cl, -1), w_out, cs, mc[2])

        wq = peer_w_q[i].astype(BF16)
        keys = peer_keys[i].astype(BF16)
        u, vt = _peer_tables(peer_u, peer_v, i)
        xs = _peer(xs, norm_ffn[i], mx[3], mx[4], mx[5], wq, keys, u, vt,
                   final_gain=norm_final if last else None)
        if not last:
            cs = _peer(cs, norm_ffn[i], mc[3], mc[4], mc[5], wq, keys, u, vt)
    return xs
```

```python
import functools
import math

import jax
import jax.numpy as jnp
from jax import lax
from jax.experimental import pallas as pl
from jax.experimental.pallas import tpu as pltpu

F32 = jnp.float32
BF16 = jnp.bfloat16

NORM_EPS = 1e-6
N_MOD = 6
GRID_W = 64
HEAD_DIM = 128
N_Q_HEADS = 16
N_KV_HEADS = 4
GQA_GROUP = N_Q_HEADS // N_KV_HEADS
ATTN_BLOCK = 128
ROPE_BASE = 10000.0
RNN_BLOCKS = 8
CONV_W = 4
CONV_LEFT = 1
LRU_C = 8.0
PEER_HEADS = 8
PEER_KEY_DIM = 128
N_KEYS = 128
PEER_TOPK = 16
PEER_EB = 512
PEER_SUB = 512
PEER_PIECE = 256

V7X_VMEM_BYTES = 64 * 1024 * 1024
SUBLANES = 8
LANES = 128
VMEM_CAP = V7X_VMEM_BYTES - 6 * 1024 * 1024


def _params(sem, vmem_bytes):
    limit = int(min(VMEM_CAP, max(32 * 1024 * 1024, vmem_bytes * 5 // 4)))
    return pltpu.CompilerParams(dimension_semantics=sem, vmem_limit_bytes=limit)


def _tile(n, pref):
    return pref if n % pref == 0 else n


def _norm_mod(x, gain, shift, scale):
    ms = jnp.mean(x * x, axis=-1, keepdims=True)
    y = x * lax.rsqrt(ms + NORM_EPS)
    return (y * gain) * (1.0 + scale) + shift


def _dot_nt(a, b):
    return lax.dot_general(a, b, (((1,), (1,)), ((), ())), preferred_element_type=F32)


def _mod_body(c_ref, w_ref, b_ref, o_ref):
    s = jax.nn.silu(c_ref[...])
    o_ref[0] = jnp.dot(s.astype(BF16), w_ref[0].astype(BF16), preferred_element_type=F32) + b_ref[0]


def _mod_vectors(cvec, w_mod, b_mod):
    depth, d, n = w_mod.shape
    r = cvec.shape[0]
    tn = _tile(n, 1536)
    vm = 2 * d * tn * 4 + d * tn * 2 + 4 * r * tn * 4
    return pl.pallas_call(
        _mod_body,
        out_shape=jax.ShapeDtypeStruct((depth, r, n), F32),
        grid=(depth, n // tn),
        in_specs=[
            pl.BlockSpec((r, d), lambda i, j: (0, 0)),
            pl.BlockSpec((1, d, tn), lambda i, j: (i, 0, j)),
            pl.BlockSpec((1, 1, tn), lambda i, j: (i, 0, j)),
        ],
        out_specs=pl.BlockSpec((1, r, tn), lambda i, j: (i, 0, j)),
        compiler_params=_params(("arbitrary", "arbitrary"), vm),
        name="mod_vectors",
    )(cvec, w_mod, b_mod.reshape(depth, 1, n))


def _rope(x, cos, sin_signed):
    lane = lax.broadcasted_iota(jnp.int32, x.shape, 1)
    qtr = HEAD_DIM // 4
    first = (lane % (2 * qtr)) < qtr
    rot = jnp.where(first, pltpu.roll(x, HEAD_DIM - qtr, 1), pltpu.roll(x, qtr, 1))
    return x * cos + rot * sin_signed


def _proj_body(*refs, tn, rope_cols):
    if rope_cols:
        x_ref, g_ref, sh_ref, sc_ref, w_ref, cos_ref, sin_ref, o_ref, h_scr = refs
    else:
        x_ref, g_ref, sh_ref, sc_ref, w_ref, o_ref, h_scr = refs
    h_scr[...] = _norm_mod(x_ref[...], g_ref[...], sh_ref[...], sc_ref[...]).astype(BF16)
    for j in range(w_ref.shape[1] // tn):
        ns = slice(j * tn, (j + 1) * tn)
        y = jnp.dot(h_scr[...], w_ref[:, ns], preferred_element_type=F32)
        if j * tn < rope_cols:
            cos = cos_ref[...]
            sin = sin_ref[...]
            y = jnp.concatenate(
                [_rope(y[:, k * HEAD_DIM:(k + 1) * HEAD_DIM], cos, sin)
                 for k in range(tn // HEAD_DIM)], axis=1)
        o_ref[:, ns] = y.astype(o_ref.dtype)


def _proj(x, gain, shift, scale, w, out_dtype, tn, rope=None):
    b, t, d = x.shape
    n = w.shape[1]
    tm = _tile(t, 512)
    in_specs = [
        pl.BlockSpec((None, tm, d), lambda bb, i: (bb, i, 0)),
        pl.BlockSpec((1, d), lambda bb, i: (0, 0)),
        pl.BlockSpec((None, 1, d), lambda bb, i: (bb, 0, 0)),
        pl.BlockSpec((None, 1, d), lambda bb, i: (bb, 0, 0)),
        pl.BlockSpec((d, n), lambda bb, i: (0, 0), pipeline_mode=pl.Buffered(1)),
    ]
    args = [x, gain.reshape(1, d), shift, scale, w]
    rope_cols = 0
    if rope is not None:
        cos, sin_signed, rope_cols = rope
        assert rope_cols % tn == 0
        in_specs += [pl.BlockSpec((tm, HEAD_DIM), lambda bb, i: (i, 0))] * 2
        args += [cos, sin_signed]
    osz = jnp.dtype(out_dtype).itemsize
    vm = 2 * tm * d * 4 + d * n * 2 + 2 * tm * n * osz + tm * d * 2 + 3 * tm * tn * 4
    return pl.pallas_call(
        functools.partial(_proj_body, tn=tn, rope_cols=rope_cols),
        out_shape=jax.ShapeDtypeStruct((b, t, n), out_dtype),
        grid=(b, t // tm),
        in_specs=in_specs,
        out_specs=pl.BlockSpec((None, tm, n), lambda bb, i: (bb, i, 0)),
        scratch_shapes=[pltpu.VMEM((tm, d), BF16)],
        compiler_params=_params(("arbitrary", "arbitrary"), vm),
        name="norm_mod_proj",
    )(*args)


def _oproj_body(a_ref, w_ref, res_ref, gate_ref, o_ref):
    y = jnp.dot(a_ref[...], w_ref[...], preferred_element_type=F32)
    o_ref[...] = res_ref[...] + gate_ref[...] * y


def _rec_oproj_body(yf_ref, yb_ref, gu_ref, w_ref, res_ref, gate_ref, o_ref):
    a = (yf_ref[...] + yb_ref[...]) * jax.nn.gelu(gu_ref[...])
    y = jnp.dot(a.astype(BF16), w_ref[...], preferred_element_type=F32)
    o_ref[...] = res_ref[...] + gate_ref[...] * y


def _oproj(a, w, res, gate):
    b, t, k = a.shape
    d = w.shape[1]
    tm = _tile(t, 512)
    vm = 2 * tm * k * 2 + 2 * k * d * 2 + 4 * tm * d * 4 + tm * d * 4
    return pl.pallas_call(
        _oproj_body,
        out_shape=jax.ShapeDtypeStruct((b, t, d), F32),
        grid=(b, t // tm),
        in_specs=[
            pl.BlockSpec((None, tm, k), lambda bb, i: (bb, i, 0)),
            pl.BlockSpec((k, d), lambda bb, i: (0, 0)),
            pl.BlockSpec((None, tm, d), lambda bb, i: (bb, i, 0)),
            pl.BlockSpec((None, 1, d), lambda bb, i: (bb, 0, 0)),
        ],
        out_specs=pl.BlockSpec((None, tm, d), lambda bb, i: (bb, i, 0)),
        compiler_params=_params(("arbitrary", "arbitrary"), vm),
        name="oproj_residual",
    )(a, w, res, gate)


def _rec_oproj(yf, yb, gu, w, res, gate):
    b, t, k = yf.shape
    d = w.shape[1]
    tm = _tile(t, 256)
    vm = 6 * tm * k * 4 + 2 * k * d * 2 + 4 * tm * d * 4 + 3 * tm * d * 4
    return pl.pallas_call(
        _rec_oproj_body,
        out_shape=jax.ShapeDtypeStruct((b, t, d), F32),
        grid=(b, t // tm),
        in_specs=[
            pl.BlockSpec((None, tm, k), lambda bb, i: (bb, i, 0)),
            pl.BlockSpec((None, tm, k), lambda bb, i: (bb, i, 0)),
            pl.BlockSpec((None, tm, k), lambda bb, i: (bb, i, 0)),
            pl.BlockSpec((k, d), lambda bb, i: (0, 0)),
            pl.BlockSpec((None, tm, d), lambda bb, i: (bb, i, 0)),
            pl.BlockSpec((None, 1, d), lambda bb, i: (bb, 0, 0)),
        ],
        out_specs=pl.BlockSpec((None, tm, d), lambda bb, i: (bb, i, 0)),
        compiler_params=_params(("arbitrary", "arbitrary"), vm),
        name="rec_oproj_residual",
    )(yf, yb, gu, w, res, gate)


def _stack_groups(q):
    return jnp.concatenate(
        [q[:, g * HEAD_DIM:(g + 1) * HEAD_DIM] for g in range(GQA_GROUP)], axis=0)


def _unstack_groups(o, rows):
    return jnp.concatenate([o[g * rows:(g + 1) * rows] for g in range(GQA_GROUP)], axis=1)


def _sink_column(sink_ref, h, rows):
    return jnp.concatenate(
        [jnp.full((rows, 1), sink_ref[h * GQA_GROUP + g], F32) for g in range(GQA_GROUP)], axis=0)


def _attn_body(sink_ref, q_ref, kp_ref, kc_ref, kn_ref, vp_ref, vc_ref, vn_ref, kx_ref, vx_ref,
               o_ref, *, nb):
    i = pl.program_id(1)
    blk = ATTN_BLOCK
    scale = HEAD_DIM ** -0.5
    r = lax.broadcasted_iota(jnp.int32, (GQA_GROUP * blk, blk), 0) % blk
    c = lax.broadcasted_iota(jnp.int32, (GQA_GROUP * blk, blk), 1)
    keep_p = (c >= r) & (i >= 1)
    keep_n = (c <= r) & (i + 1 < nb)
    neg = -jnp.inf
    qw = GQA_GROUP * HEAD_DIM
    for h in range(N_KV_HEADS):
        hs = slice(h * HEAD_DIM, (h + 1) * HEAD_DIM)
        qs = _stack_groups(q_ref[:, h * qw:(h + 1) * qw])
        s_p = jnp.where(keep_p, _dot_nt(qs, kp_ref[:, hs]) * scale, neg)
        s_c = _dot_nt(qs, kc_ref[:, hs]) * scale
        s_n = jnp.where(keep_n, _dot_nt(qs, kn_ref[:, hs]) * scale, neg)
        s_x = _dot_nt(qs, kx_ref[:, hs]) * scale
        sink = _sink_column(sink_ref, h, blk)
        m = jnp.maximum(
            jnp.maximum(jnp.maximum(s_p.max(-1, keepdims=True), s_c.max(-1, keepdims=True)),
                        jnp.maximum(s_n.max(-1, keepdims=True), s_x.max(-1, keepdims=True))),
            sink)
        p_p = jnp.exp(s_p - m)
        p_c = jnp.exp(s_c - m)
        p_n = jnp.exp(s_n - m)
        p_x = jnp.exp(s_x - m)
        denom = (p_p.sum(-1, keepdims=True) + p_c.sum(-1, keepdims=True)
                 + p_n.sum(-1, keepdims=True) + p_x.sum(-1, keepdims=True) + jnp.exp(sink - m))
        o = (jnp.dot(p_p.astype(BF16), vp_ref[:, hs], preferred_element_type=F32)
             + jnp.dot(p_c.astype(BF16), vc_ref[:, hs], preferred_element_type=F32)
             + jnp.dot(p_n.astype(BF16), vn_ref[:, hs], preferred_element_type=F32)
             + jnp.dot(p_x.astype(BF16), vx_ref[:, hs], preferred_element_type=F32))
        o_ref[:, h * qw:(h + 1) * qw] = _unstack_groups(o / denom, blk).astype(o_ref.dtype)


def _attention(qkv_x, qkv_c, sink):
    b, s, _ = qkv_x.shape
    c = qkv_c.shape[1]
    nb = s // ATTN_BLOCK
    q_cols = N_Q_HEADS * HEAD_DIM
    kv_cols = N_KV_HEADS * HEAD_DIM
    k0 = q_cols // kv_cols
    blk = ATTN_BLOCK

    def kv_spec(col, off):
        return pl.BlockSpec((None, blk, kv_cols),
                            lambda bb, i: (bb, jnp.clip(i + off, 0, nb - 1), col))

    in_specs = [
        pl.BlockSpec(memory_space=pltpu.SMEM),
        pl.BlockSpec((None, blk, q_cols), lambda bb, i: (bb, i, 0)),
        kv_spec(k0, -1), kv_spec(k0, 0), kv_spec(k0, 1),
        kv_spec(k0 + 1, -1), kv_spec(k0 + 1, 0), kv_spec(k0 + 1, 1),
        pl.BlockSpec((None, c, kv_cols), lambda bb, i: (bb, 0, k0)),
        pl.BlockSpec((None, c, kv_cols), lambda bb, i: (bb, 0, k0 + 1)),
    ]
    return pl.pallas_call(
        functools.partial(_attn_body, nb=nb),
        out_shape=jax.ShapeDtypeStruct((b, s, q_cols), BF16),
        grid=(b, nb),
        in_specs=in_specs,
        out_specs=pl.BlockSpec((None, blk, q_cols), lambda bb, i: (bb, i, 0)),
        compiler_params=_params(("arbitrary", "arbitrary"), 24 * 1024 * 1024),
        name="window_attention",
    )(sink, qkv_x, qkv_x, qkv_x, qkv_x, qkv_x, qkv_x, qkv_x, qkv_c, qkv_c)


def _ctx_attn_body(sink_ref, q_ref, k_ref, v_ref, o_ref):
    h = pl.program_id(1)
    rows = q_ref.shape[0]
    scale = HEAD_DIM ** -0.5
    qs = _stack_groups(q_ref[...])
    s = _dot_nt(qs, k_ref[...]) * scale
    sink = _sink_column(sink_ref, h, rows)
    m = jnp.maximum(s.max(-1, keepdims=True), sink)
    p = jnp.exp(s - m)
    denom = p.sum(-1, keepdims=True) + jnp.exp(sink - m)
    o = jnp.dot((p / denom).astype(BF16), v_ref[...], preferred_element_type=F32)
    o_ref[...] = _unstack_groups(o, rows).astype(o_ref.dtype)


def _ctx_attention(qkv_c, sink):
    b, c, _ = qkv_c.shape
    qw = GQA_GROUP * HEAD_DIM
    k0 = N_Q_HEADS
    v0 = N_Q_HEADS + N_KV_HEADS
    return pl.pallas_call(
        _ctx_attn_body,
        out_shape=jax.ShapeDtypeStruct((b, c, N_Q_HEADS * HEAD_DIM), BF16),
        grid=(b, N_KV_HEADS),
        in_specs=[
            pl.BlockSpec(memory_space=pltpu.SMEM),
            pl.BlockSpec((None, c, qw), lambda bb, h: (bb, 0, h)),
            pl.BlockSpec((None, c, HEAD_DIM), lambda bb, h: (bb, 0, k0 + h)),
            pl.BlockSpec((None, c, HEAD_DIM), lambda bb, h: (bb, 0, v0 + h)),
        ],
        out_specs=pl.BlockSpec((None, c, qw), lambda bb, h: (bb, 0, h)),
        compiler_params=_params(("arbitrary", "arbitrary"), 16 * 1024 * 1024),
        name="context_attention",
    )(sink, qkv_c, qkv_c, qkv_c)


def _block_diag(ub, w_ref, bias):
    bw = w_ref.shape[1]
    return jnp.concatenate(
        [jnp.dot(ub[:, n * bw:(n + 1) * bw], w_ref[n], preferred_element_type=F32)
         for n in range(w_ref.shape[0])], axis=1) + bias


def _lru_body(up_ref, uc_ref, un_ref, cw_ref, cb_ref, wa_ref, ba_ref, wx_ref, bx_ref, lam_ref,
              h0_ref, y_ref, hl_ref, h_scr, a_scr, b_scr, *, nt, reverse):
    i = pl.program_id(1)
    ti = (nt - 1 - i) if reverse else i
    tm = uc_ref.shape[0]
    halo = up_ref.shape[0]

    @pl.when(i == 0)
    def _():
        h_scr[...] = h0_ref[...]

    prev = jnp.where(ti > 0, up_ref[...], 0.0)
    nxt = jnp.where(ti < nt - 1, un_ref[...], 0.0)
    cur = uc_ref[...]

    def taps(window, rows):
        y = cb_ref[...]
        for k in range(CONV_W):
            off = halo - CONV_LEFT + k
            y = y + window[off:off + rows] * cw_ref[k:k + 1, :]
        return y

    u = cb_ref[...]
    for k in range(CONV_W):
        shift = (CONV_LEFT - k) % tm
        u = u + (cur if shift == 0 else pltpu.roll(cur, shift, 0)) * cw_ref[k:k + 1, :]
    u_lo = taps(jnp.concatenate([prev, cur[:2 * halo]], axis=0), halo)
    u_hi = taps(jnp.concatenate([cur[tm - 2 * halo:], nxt], axis=0), halo)
    u = jnp.concatenate([u_lo, u[halo:tm - halo], u_hi], axis=0)

    ub = u.astype(BF16)
    r = jax.nn.sigmoid(_block_diag(ub, wa_ref, ba_ref[...]))
    ig = jax.nn.sigmoid(_block_diag(ub, wx_ref, bx_ref[...]))
    nl = -lam_ref[...]
    softplus = jnp.maximum(nl, 0.0) + jnp.log1p(jnp.exp(-jnp.abs(nl)))
    log_a = -LRU_C * r * softplus
    a_scr[...] = jnp.exp(log_a)
    b_scr[...] = jnp.sqrt(1.0 - jnp.exp(2.0 * log_a)) * ig * u

    ng = tm // SUBLANES
    row = lax.broadcasted_iota(jnp.int32, (SUBLANES, a_scr.shape[1]), 0)

    def group(g, h):
        gi = (ng - 1 - g) if reverse else g
        r0 = pl.multiple_of(gi * SUBLANES, SUBLANES)
        a = a_scr[pl.ds(r0, SUBLANES), :]
        bb = b_scr[pl.ds(r0, SUBLANES), :]
        for k in (1, 2, 4):
            if reverse:
                keep = row < SUBLANES - k
                shift = SUBLANES - k
            else:
                keep = row >= k
                shift = k
            a_sh = pltpu.roll(a, shift, 0)
            b_sh = pltpu.roll(bb, shift, 0)
            bb = bb + a * jnp.where(keep, b_sh, 0.0)
            a = a * jnp.where(keep, a_sh, 1.0)
        y = bb + a * h
        y_ref[pl.ds(r0, SUBLANES), :] = y
        return y[0:1, :] if reverse else y[SUBLANES - 1:SUBLANES, :]

    h_last = lax.fori_loop(0, ng, group, h_scr[...])
    h_scr[...] = h_last

    @pl.when(i == nt - 1)
    def _():
        hl_ref[...] = h_last


def _lru_scan(gu, conv_w, conv_b, w_a, b_a, w_x, b_x, lam, h0, reverse):
    b, t, d2 = gu.shape
    d = d2 // 2
    tm = _tile(t, 256)
    nt = t // tm
    halo = SUBLANES
    hb = tm // halo
    nh = t // halo

    def tmap(i):
        return (nt - 1 - i) if reverse else i

    vec = lambda: pl.BlockSpec((1, d), lambda bb, i: (0, 0))
    in_specs = [
        pl.BlockSpec((None, halo, d), lambda bb, i: (bb, jnp.maximum(tmap(i) * hb - 1, 0), 1)),
        pl.BlockSpec((None, tm, d), lambda bb, i: (bb, tmap(i), 1)),
        pl.BlockSpec((None, halo, d), lambda bb, i: (bb, jnp.minimum((tmap(i) + 1) * hb, nh - 1), 1)),
        pl.BlockSpec((CONV_W, d), lambda bb, i: (0, 0)),
        vec(),
        pl.BlockSpec(w_a.shape, lambda bb, i: (0, 0, 0)),
        vec(),
        pl.BlockSpec(w_x.shape, lambda bb, i: (0, 0, 0)),
        vec(),
        vec(),
        pl.BlockSpec((None, 1, d), lambda bb, i: (bb, 0, 0)),
    ]
    vm = 2 * (tm + 2 * halo) * d * 4 + 2 * tm * d * 4 + 2 * tm * d * 4 + 8 * tm * d * 4 + 4 * w_a.size * 2
    return pl.pallas_call(
        functools.partial(_lru_body, nt=nt, reverse=reverse),
        out_shape=(jax.ShapeDtypeStruct((b, t, d), F32), jax.ShapeDtypeStruct((b, 1, d), F32)),
        grid=(b, nt),
        in_specs=in_specs,
        out_specs=(pl.BlockSpec((None, tm, d), lambda bb, i: (bb, tmap(i), 0)),
                   pl.BlockSpec((None, 1, d), lambda bb, i: (bb, 0, 0))),
        scratch_shapes=[pltpu.VMEM((1, d), F32), pltpu.VMEM((tm, d), F32), pltpu.VMEM((tm, d), F32)],
        compiler_params=_params(("arbitrary", "arbitrary"), vm),
        name="rglru_bwd" if reverse else "rglru_fwd",
    )(gu, gu, gu, conv_w, conv_b.reshape(1, d), w_a, b_a.reshape(1, d), w_x, b_x.reshape(1, d),
      lam.reshape(1, d), h0)


def _peer_query_body(x_ref, g_ref, sh_ref, sc_ref, wq_ref, keys_ref, ht_ref, st_ref):
    h = _norm_mod(x_ref[...], g_ref[...], sh_ref[...], sc_ref[...])
    hb = h.astype(BF16)
    ht = h.T.astype(BF16)
    for p in range(ht_ref.shape[0]):
        ht_ref[p] = ht[:, p * PEER_PIECE:(p + 1) * PEER_PIECE]
    q = jnp.dot(hb, wq_ref[...], preferred_element_type=F32).astype(BF16)
    for hp in range(2 * PEER_HEADS):
        qc = q[:, hp * PEER_KEY_DIM:(hp + 1) * PEER_KEY_DIM]
        st_ref[hp] = _dot_nt(keys_ref[hp % 2], qc)


def _peer_query(x, gain, shift, scale, wq, keys):
    b, t, d = x.shape
    tm = _tile(t, PEER_SUB)
    nt = t // tm
    nq = wq.shape[1]
    vm = 2 * tm * d * 4 + 2 * d * nq * 2 + 2 * d * tm * 2 + 2 * 2 * PEER_HEADS * N_KEYS * tm * 4 + 4 * tm * d * 4
    return pl.pallas_call(
        _peer_query_body,
        out_shape=(jax.ShapeDtypeStruct((b * nt, tm // PEER_PIECE, d, PEER_PIECE), BF16),
                   jax.ShapeDtypeStruct((2 * PEER_HEADS, N_KEYS, b * t), F32)),
        grid=(b, nt),
        in_specs=[
            pl.BlockSpec((None, tm, d), lambda bb, i: (bb, i, 0)),
            pl.BlockSpec((1, d), lambda bb, i: (0, 0)),
            pl.BlockSpec((None, 1, d), lambda bb, i: (bb, 0, 0)),
            pl.BlockSpec((None, 1, d), lambda bb, i: (bb, 0, 0)),
            pl.BlockSpec((d, nq), lambda bb, i: (0, 0)),
            pl.BlockSpec(keys.shape, lambda bb, i: (0, 0, 0)),
        ],
        out_specs=(pl.BlockSpec((None, tm // PEER_PIECE, d, PEER_PIECE),
                                lambda bb, i: (bb * nt + i, 0, 0, 0)),
                   pl.BlockSpec((2 * PEER_HEADS, N_KEYS, tm), lambda bb, i: (0, 0, bb * nt + i))),
        compiler_params=_params(("arbitrary", "arbitrary"), vm),
        name="peer_query",
    )(x, gain.reshape(1, d), shift, scale, wq, keys)


def _oddeven_merge(lo, hi, r):
    step = r * 2
    if step < hi - lo:
        yield from _oddeven_merge(lo, hi, step)
        yield from _oddeven_merge(lo + r, hi, step)
        yield from [(k, k + r) for k in range(lo + r, hi - r, step)]
    else:
        yield (lo, lo + r)


def _oddeven_sort(lo, hi):
    if hi - lo >= 1:
        mid = lo + (hi - lo) // 2
        yield from _oddeven_sort(lo, mid)
        yield from _oddeven_sort(mid + 1, hi)
        yield from _oddeven_merge(lo, hi, 1)


_SORT16 = tuple(_oddeven_sort(0, PEER_TOPK - 1))


def _exchange(x, p, q):
    hi = jnp.maximum(x[p], x[q])
    lo = jnp.minimum(x[p], x[q])
    x[p] = hi
    x[q] = lo


def _merge_sublanes(x):
    n = len(x)
    shift = SUBLANES // 2
    while shift >= 1:
        z = [jnp.maximum(x[k], pltpu.roll(x[n - 1 - k], shift, 0)) for k in range(n)]
        dist = n // 2
        while dist >= 1:
            for k in range(n):
                if k & dist == 0:
                    _exchange(z, k, k + dist)
            dist //= 2
        x = z
        shift //= 2
    return x


def _top16_sorted(s):
    x = [s[SUBLANES * v:SUBLANES * (v + 1)] for v in range(s.shape[0] // SUBLANES)]
    assert len(x) == PEER_TOPK
    for p, q in _SORT16:
        _exchange(x, p, q)
    return _merge_sublanes(x)


def _dup16(v):
    bits = pltpu.bitcast(v.astype(BF16).astype(F32), jnp.uint32) >> 16
    return bits | (bits << 16)


def _route_stats(st_ref, e1d_scr, c1d_scr, e2_scr, r2_scr):
    tm = st_ref.shape[2]
    row = lax.broadcasted_iota(jnp.int32, (SUBLANES, tm), 0)
    for h in range(PEER_HEADS):
        s1 = st_ref[2 * h]
        s2 = st_ref[2 * h + 1]
        a = _top16_sorted(s1)
        b = _top16_sorted(s2)
        a_lo = a[SUBLANES - 1]
        a_hi = a[2 * SUBLANES - 1]
        for i in range(SUBLANES - 2, -1, -1):
            a_lo = jnp.where(row == i, a[i], a_lo)
            a_hi = jnp.where(row == i, a[SUBLANES + i], a_hi)
        c = [a_lo + b[j] for j in range(PEER_TOPK)]
        d = a_hi + b[0]
        c = [jnp.maximum(c[0], d)] + [
            jnp.maximum(c[j], jnp.minimum(c[j - 1], d)) for j in range(1, PEER_TOPK)]
        t = _merge_sublanes(c)
        z = jnp.ones_like(t[0])
        for k in range(1, PEER_TOPK):
            z = z + jnp.exp(t[k] - t[0])
        inv_z = 1.0 / z
        tau = t[PEER_TOPK - 1][0:1]
        count1 = jnp.zeros_like(s1)
        rank2 = jnp.zeros_like(s2)
        for j in range(PEER_TOPK):
            bj = b[j][0:1]
            count1 = count1 + jnp.where(s1 + bj >= tau, 1.0, 0.0)
            rank2 = rank2 + jnp.where(bj > s2, 1.0, 0.0)
        e1 = jnp.exp(s1 - a[0][0:1]) * inv_z[0:1]
        e2 = jnp.exp(s2 - b[0][0:1])
        for cc in range(tm // LANES):
            cs = slice(cc * LANES, (cc + 1) * LANES)
            c1d_scr[h, cc] = _dup16(count1[:, cs])
            e1d_scr[h, cc] = _dup16(e1[:, cs])
            r2_scr[h, cc] = rank2[:, cs].astype(BF16)
            e2_scr[h, cc] = e2[:, cs].astype(BF16)


def _route_body(st_ref, e1d_ref, c1d_ref, e2_ref, r2_ref):
    _route_stats(st_ref, e1d_ref, c1d_ref, e2_ref, r2_ref)


def _peer_route(st):
    t = st.shape[2]
    tm = _tile(t, PEER_SUB)
    nc = tm // LANES
    shape = (PEER_HEADS, t // LANES, N_KEYS, LANES)
    spec = pl.BlockSpec((PEER_HEADS, nc, N_KEYS, LANES), lambda i: (0, i, 0, 0))
    stat = PEER_HEADS * N_KEYS * tm
    return pl.pallas_call(
        _route_body,
        out_shape=(jax.ShapeDtypeStruct(shape, jnp.uint32), jax.ShapeDtypeStruct(shape, jnp.uint32),
                   jax.ShapeDtypeStruct(shape, BF16), jax.ShapeDtypeStruct(shape, BF16)),
        grid=(t // tm,),
        in_specs=[pl.BlockSpec((2 * PEER_HEADS, N_KEYS, tm), lambda i: (0, 0, i))],
        out_specs=(spec, spec, spec, spec),
        compiler_params=_params(("arbitrary",), 2 * 2 * stat * 4 + 2 * stat * 12 + 8 * stat),
        name="peer_route",
    )(st)


def _dense_act(zt_ref, act_ref, blk, c0, cols, e1d_ref, c1d_ref, e2_ref, r2_ref):
    eb = zt_ref.shape[1]
    for l in range(eb // N_KEYS):
        i1 = jnp.clip(blk * (eb // N_KEYS) + l, 0, N_KEYS - 1)
        rs = slice(l * N_KEYS, (l + 1) * N_KEYS)
        for c in cols:
            g = None
            for h in range(PEER_HEADS):
                cnt = jnp.broadcast_to(c1d_ref[h, c0 + c, pl.ds(i1, 1), :], (N_KEYS // 2, LANES))
                e1 = jnp.broadcast_to(e1d_ref[h, c0 + c, pl.ds(i1, 1), :], (N_KEYS // 2, LANES))
                gh = jnp.where(r2_ref[h, c0 + c] < pltpu.bitcast(cnt, BF16),
                               e2_ref[h, c0 + c] * pltpu.bitcast(e1, BF16), jnp.zeros((), BF16))
                g = gh if g is None else g + gh
            act_ref[c, rs, :] = jax.nn.gelu(zt_ref[c, rs, :].astype(BF16)) * g


def _peer_expert_body(ht_ref, e1d_hbm, c1d_hbm, e2_hbm, r2_hbm, u_ref, vt_ref, o_ref,
                      e1d_scr, c1d_scr, e2_scr, r2_scr, gate_sem, zt_scr, act_scr, *, eb):
    j = pl.program_id(1)
    ns = pl.num_programs(1) - 1
    gates = (e1d_scr, c1d_scr, e2_scr, r2_scr)
    n_sub, n_pc, _, pw = ht_ref.shape
    cpp = pw // LANES
    nc = n_pc * cpp

    def cols_of(p):
        return range(p * cpp, (p + 1) * cpp)

    def stage_a(k, s, p):
        z = jnp.dot(u_ref[s * eb:(s + 1) * eb, :], ht_ref[k, p], preferred_element_type=F32)
        for n, c in enumerate(cols_of(p)):
            zt_scr[k, s, c] = z[:, n * LANES:(n + 1) * LANES]

    def stage_b(k, s, p):
        _dense_act(zt_scr.at[k, 1 - s], act_scr.at[k, 1 - s], 2 * j - 1 + s, k * nc, cols_of(p),
                   *gates)

    def stage_c(k, s, p):
        act = jnp.concatenate([act_scr[k, s, c] for c in cols_of(p)], axis=1)
        o_ref[k, p] += jnp.dot(vt_ref[s], act, preferred_element_type=F32)

    def first_step(k, carry):
        for p in range(n_pc):
            stage_a(k, 0, p)
        for p in range(n_pc):
            stage_b(k, 1, p)
            stage_a(k, 1, p)
        return carry

    def middle_step(k, carry):
        for s in (0, 1):
            for p in range(n_pc):
                stage_c(k, s, p)
                stage_b(k, s, p)
                stage_a(k, s, p)
        return carry

    def last_step(k, carry):
        for p in range(n_pc):
            stage_c(k, 0, p)
            stage_b(k, 0, p)
        for p in range(n_pc):
            stage_c(k, 1, p)
        return carry

    @pl.when(j == 0)
    def _():
        gnc = e1d_scr.shape[1]
        first = pl.multiple_of(pl.program_id(0) * gnc, gnc)
        copies = [
            pltpu.make_async_copy(src.at[:, pl.ds(first, gnc)], dst, gate_sem.at[n])
            for n, (src, dst) in enumerate(zip((e1d_hbm, c1d_hbm, e2_hbm, r2_hbm), gates))]
        for cp in copies:
            cp.start()
        o_ref[...] = jnp.zeros_like(o_ref)
        for cp in copies:
            cp.wait()
        lax.fori_loop(0, n_sub, first_step, 0)

    @pl.when((j > 0) & (j < ns))
    def _():
        lax.fori_loop(0, n_sub, middle_step, 0)

    @pl.when(j == ns)
    def _():
        lax.fori_loop(0, n_sub, last_step, 0)


def _peer_experts(ht, gates, u, vt):
    n_tiles, n_pc, d, pw = ht.shape
    tm = n_pc * pw
    e = u.shape[0]
    g = 2 if n_tiles % 2 == 0 else 1
    nc = tm // LANES
    eb = PEER_EB
    ns = e // (2 * eb)
    assert vt.shape == (ns, 2, d, eb)
    once = pl.Buffered(1)
    gate_spec = pl.BlockSpec(memory_space=pl.ANY)
    gate_shape = (PEER_HEADS, g * nc, N_KEYS, LANES)
    in_specs = [
        pl.BlockSpec((g, n_pc, d, pw), lambda i, j: (i, 0, 0, 0), pipeline_mode=once),
        gate_spec, gate_spec, gate_spec, gate_spec,
        pl.BlockSpec((2 * eb, d), lambda i, j: (jnp.minimum(j, ns - 1), 0)),
        pl.BlockSpec((None, 2, d, eb), lambda i, j: (jnp.maximum(j - 1, 0), 0, 0, 0)),
    ]
    stat = PEER_HEADS * N_KEYS * tm
    vm = g * (d * tm * 2 + stat * 12 + 2 * d * tm * 4 + 2 * eb * tm * 6) + 8 * eb * d * 2
    return pl.pallas_call(
        functools.partial(_peer_expert_body, eb=eb),
        out_shape=jax.ShapeDtypeStruct((n_tiles, n_pc, d, pw), F32),
        grid=(n_tiles // g, ns + 1),
        in_specs=in_specs,
        out_specs=pl.BlockSpec((g, n_pc, d, pw), lambda i, j: (i, 0, 0, 0)),
        scratch_shapes=[
            pltpu.VMEM(gate_shape, jnp.uint32),
            pltpu.VMEM(gate_shape, jnp.uint32),
            pltpu.VMEM(gate_shape, BF16),
            pltpu.VMEM(gate_shape, BF16),
            pltpu.SemaphoreType.DMA((4,)),
            pltpu.VMEM((g, 2, nc, eb, LANES), F32),
            pltpu.VMEM((g, 2, nc, eb, LANES), BF16),
        ],
        compiler_params=_params(("arbitrary", "arbitrary"), vm),
        name="peer_experts",
    )(ht, *gates, u, vt)


def _peer_residual_body(*refs, final_norm):
    if final_norm:
        xs_ref, gate_ref, ft_ref, gain_ref, o_ref = refs
    else:
        xs_ref, gate_ref, ft_ref, o_ref = refs
    f = jnp.concatenate([ft_ref[p].T for p in range(ft_ref.shape[0])], axis=0)
    y = xs_ref[...] + gate_ref[...] * f
    if final_norm:
        ms = jnp.mean(y * y, axis=-1, keepdims=True)
        y = (y * lax.rsqrt(ms + NORM_EPS)) * gain_ref[...]
    o_ref[...] = y


def _peer_residual(xs, gate, ft, final_gain=None):
    b, t, d = xs.shape
    n_pc, pw = ft.shape[1], ft.shape[3]
    tm = n_pc * pw
    nt = t // tm
    final_norm = final_gain is not None
    in_specs = [
        pl.BlockSpec((None, tm, d), lambda bb, i: (bb, i, 0)),
        pl.BlockSpec((None, 1, d), lambda bb, i: (bb, 0, 0)),
        pl.BlockSpec((None, n_pc, d, pw), lambda bb, i: (bb * nt + i, 0, 0, 0)),
    ]
    args = [xs, gate, ft]
    if final_norm:
        in_specs.append(pl.BlockSpec((1, d), lambda bb, i: (0, 0)))
        args.append(final_gain.reshape(1, d))
    return pl.pallas_call(
        functools.partial(_peer_residual_body, final_norm=final_norm),
        out_shape=jax.ShapeDtypeStruct((b, t, d), F32),
        grid=(b, nt),
        in_specs=in_specs,
        out_specs=pl.BlockSpec((None, tm, d), lambda bb, i: (bb, i, 0)),
        compiler_params=_params(("arbitrary", "arbitrary"), 8 * tm * d * 4),
        name="peer_residual",
    )(*args)


def _tables_body(u_ref, v_ref, ub_ref, vt_ref):
    ub_ref[...] = u_ref[...].astype(BF16)
    vt_ref[...] = v_ref[...].T.astype(BF16)


def _peer_tables(u, v, layer):
    _, e, d = u.shape
    eb = PEER_EB
    table = pl.BlockSpec((None, eb, d), lambda i: (layer, i, 0))
    return pl.pallas_call(
        _tables_body,
        out_shape=(jax.ShapeDtypeStruct((e, d), BF16),
                   jax.ShapeDtypeStruct((e // (2 * eb), 2, d, eb), BF16)),
        grid=(e // eb,),
        in_specs=[table, table],
        out_specs=(pl.BlockSpec((eb, d), lambda i: (i, 0)),
                   pl.BlockSpec((None, None, d, eb), lambda i: (i // 2, i % 2, 0, 0))),
        compiler_params=_params(("arbitrary",), 2 * 2 * eb * d * 4 + 2 * 2 * eb * d * 2 + 2 * eb * d * 4),
        name="peer_tables",
    )(u, v)


def _peer(xs, gain, shift, scale, gate, wq, keys, u, vt, final_gain=None):
    ht, st = _peer_query(xs, gain, shift, scale, wq, keys)
    ft = _peer_experts(ht, _peer_route(st), u, vt)
    return _peer_residual(xs, gate, ft, final_gain)


def _rope_tables(s):
    t = jnp.arange(s)
    row = (t // GRID_W).astype(F32)
    col = (t % GRID_W).astype(F32)
    half = HEAD_DIM // 2
    inv = ROPE_BASE ** (-jnp.arange(0, half, 2, dtype=F32) / half)
    ang_r = row[:, None] * inv[None, :]
    ang_c = col[:, None] * inv[None, :]
    ang = jnp.concatenate([ang_r, ang_r, ang_c, ang_c], axis=-1)
    lane = jnp.arange(HEAD_DIM)
    sign = jnp.where((lane % half) < half // 2, -1.0, 1.0).astype(F32)
    return jnp.cos(ang), jnp.sin(ang) * sign[None, :]


def kernel(x, c, ctx, c_ctx, w_mod, b_mod, norm_mix, norm_ffn, norm_final, attn_w_qkv, attn_w_o, attn_sink, rec_w_in, rec_conv_w, rec_conv_b, rec_w_a, rec_b_a, rec_w_x, rec_b_x, rec_lambda, rec_w_out, peer_w_q, peer_keys, peer_u, peer_v):
    b, s, d = x.shape
    cl = ctx.shape[1]
    depth = w_mod.shape[0]
    rows = -(-(b + 1) // SUBLANES) * SUBLANES
    cvec = jnp.zeros((rows, d), F32).at[:b].set(c).at[b].set(c_ctx)
    mod = _mod_vectors(cvec, w_mod, b_mod).reshape(depth, rows, N_MOD, d)
    cos, sin_signed = _rope_tables(s)
    q_cols = N_Q_HEADS * HEAD_DIM
    kv_cols = N_KV_HEADS * HEAD_DIM

    xs = x
    cs = ctx.reshape(1, b * cl, d)
    for i in range(depth):
        last = i == depth - 1
        j = i // 2
        mx = [mod[i, :b, k][:, None, :] for k in range(N_MOD)]
        mc = [mod[i, b:b + 1, k][:, None, :] for k in range(N_MOD)]

        if i % 2 == 0:
            wqkv = attn_w_qkv[j].astype(BF16)
            wo = attn_w_o[j].astype(BF16)
            qkv_x = _proj(xs, norm_mix[i], mx[0], mx[1], wqkv, BF16, 512,
                          rope=(cos, sin_signed, q_cols + kv_cols))
            qkv_c = _proj(cs, norm_mix[i], mc[0], mc[1], wqkv, BF16, 512).reshape(b, cl, -1)
            o_x = _attention(qkv_x, qkv_c, attn_sink[j])
            xs = _oproj(o_x, wo, xs, mx[2])
            if not last:
                o_c = _ctx_attention(qkv_c, attn_sink[j]).reshape(1, b * cl, q_cols)
                cs = _oproj(o_c, wo, cs, mc[2])
        else:
            w_in = rec_w_in[j].astype(BF16)
            w_out = rec_w_out[j].astype(BF16)
            w_a = rec_w_a[j].astype(BF16)
            w_x = rec_w_x[j].astype(BF16)
            gu_x = _proj(xs, norm_mix[i], mx[0], mx[1], w_in, F32, 1024)
            gu_c = _proj(cs, norm_mix[i], mc[0], mc[1], w_in, F32, 1024).reshape(b, cl, -1)
            h0 = jnp.zeros((b, 1, d), F32)
            ys_c, ys_x = [], []
            for r in range(2):
                lru = functools.partial(
                    _lru_scan, conv_w=rec_conv_w[j], conv_b=rec_conv_b[j], w_a=w_a[r],
                    b_a=rec_b_a[j, r], w_x=w_x[r], b_x=rec_b_x[j, r], lam=rec_lambda[j, r],
                    reverse=(r == 1))
                y_c, h_c = lru(gu_c, h0=h0)
                y_x, _ = lru(gu_x, h0=h_c)
                ys_c.append(y_c)
                ys_x.append(y_x)
            xs = _rec_oproj(ys_x[0], ys_x[1], gu_x, w_out, xs, mx[2])
            if not last:
                cs = _rec_oproj(ys_c[0].reshape(1, b * cl, d), ys_c[1].reshape(1, b * cl, d),
                                gu_c.reshape(1, b * cl, -1), w_out, cs, mc[2])

        wq = peer_w_q[i].astype(BF16)
        keys = peer_keys[i].astype(BF16)
        u, vt = _peer_tables(peer_u, peer_v, i)
        xs = _peer(xs, norm_ffn[i], mx[3], mx[4], mx[5], wq, keys, u, vt,
                   final_gain=norm_final if last else None)
        if not last:
            cs = _peer(cs, norm_ffn[i], mc[3], mc[4], mc[5], wq, keys, u, vt)
    return xs
```

```python
import functools
import math

import jax
import jax.numpy as jnp
from jax import lax
from jax.experimental import pallas as pl
from jax.experimental.pallas import tpu as pltpu

F32 = jnp.float32
BF16 = jnp.bfloat16

NORM_EPS = 1e-6
N_MOD = 6
GRID_W = 64
HEAD_DIM = 128
N_Q_HEADS = 16
N_KV_HEADS = 4
GQA_GROUP = N_Q_HEADS // N_KV_HEADS
ATTN_BLOCK = 128
ROPE_BASE = 10000.0
RNN_BLOCKS = 8
CONV_W = 4
CONV_LEFT = 1
LRU_C = 8.0
PEER_HEADS = 8
PEER_KEY_DIM = 128
N_KEYS = 128
PEER_TOPK = 16
PEER_EB = 256
PEER_SUB = 512
PEER_PIECE = 256

V7X_VMEM_BYTES = 64 * 1024 * 1024
SUBLANES = 8
LANES = 128
VMEM_CAP = V7X_VMEM_BYTES - 6 * 1024 * 1024


def _params(sem, vmem_bytes):
    limit = int(min(VMEM_CAP, max(32 * 1024 * 1024, vmem_bytes * 5 // 4)))
    return pltpu.CompilerParams(dimension_semantics=sem, vmem_limit_bytes=limit)


def _tile(n, pref):
    return pref if n % pref == 0 else n


def _norm_mod(x, gain, shift, scale):
    ms = jnp.mean(x * x, axis=-1, keepdims=True)
    y = x * lax.rsqrt(ms + NORM_EPS)
    return (y * gain) * (1.0 + scale) + shift


def _dot_nt(a, b):
    return lax.dot_general(a, b, (((1,), (1,)), ((), ())), preferred_element_type=F32)


def _mod_body(c_ref, w_ref, b_ref, o_ref):
    s = jax.nn.silu(c_ref[...])
    o_ref[0] = jnp.dot(s.astype(BF16), w_ref[0].astype(BF16), preferred_element_type=F32) + b_ref[0]


def _mod_vectors(cvec, w_mod, b_mod):
    depth, d, n = w_mod.shape
    r = cvec.shape[0]
    tn = _tile(n, 1536)
    vm = 2 * d * tn * 4 + d * tn * 2 + 4 * r * tn * 4
    return pl.pallas_call(
        _mod_body,
        out_shape=jax.ShapeDtypeStruct((depth, r, n), F32),
        grid=(depth, n // tn),
        in_specs=[
            pl.BlockSpec((r, d), lambda i, j: (0, 0)),
            pl.BlockSpec((1, d, tn), lambda i, j: (i, 0, j)),
            pl.BlockSpec((1, 1, tn), lambda i, j: (i, 0, j)),
        ],
        out_specs=pl.BlockSpec((1, r, tn), lambda i, j: (i, 0, j)),
        compiler_params=_params(("arbitrary", "arbitrary"), vm),
        name="mod_vectors",
    )(cvec, w_mod, b_mod.reshape(depth, 1, n))


def _rope(x, cos, sin_signed):
    lane = lax.broadcasted_iota(jnp.int32, x.shape, 1)
    qtr = HEAD_DIM // 4
    first = (lane % (2 * qtr)) < qtr
    rot = jnp.where(first, pltpu.roll(x, HEAD_DIM - qtr, 1), pltpu.roll(x, qtr, 1))
    return x * cos + rot * sin_signed


def _proj_body(*refs, tn, rope_cols):
    if rope_cols:
        x_ref, g_ref, sh_ref, sc_ref, w_ref, cos_ref, sin_ref, o_ref, h_scr = refs
    else:
        x_ref, g_ref, sh_ref, sc_ref, w_ref, o_ref, h_scr = refs
    h_scr[...] = _norm_mod(x_ref[...], g_ref[...], sh_ref[...], sc_ref[...]).astype(BF16)
    for j in range(w_ref.shape[1] // tn):
        ns = slice(j * tn, (j + 1) * tn)
        y = jnp.dot(h_scr[...], w_ref[:, ns], preferred_element_type=F32)
        if j * tn < rope_cols:
            cos = cos_ref[...]
            sin = sin_ref[...]
            y = jnp.concatenate(
                [_rope(y[:, k * HEAD_DIM:(k + 1) * HEAD_DIM], cos, sin)
                 for k in range(tn // HEAD_DIM)], axis=1)
        o_ref[:, ns] = y.astype(o_ref.dtype)


def _proj(x, gain, shift, scale, w, out_dtype, tn, rope=None):
    b, t, d = x.shape
    n = w.shape[1]
    tm = _tile(t, 512)
    in_specs = [
        pl.BlockSpec((None, tm, d), lambda bb, i: (bb, i, 0)),
        pl.BlockSpec((1, d), lambda bb, i: (0, 0)),
        pl.BlockSpec((None, 1, d), lambda bb, i: (bb, 0, 0)),
        pl.BlockSpec((None, 1, d), lambda bb, i: (bb, 0, 0)),
        pl.BlockSpec((d, n), lambda bb, i: (0, 0), pipeline_mode=pl.Buffered(1)),
    ]
    args = [x, gain.reshape(1, d), shift, scale, w]
    rope_cols = 0
    if rope is not None:
        cos, sin_signed, rope_cols = rope
        assert rope_cols % tn == 0
        in_specs += [pl.BlockSpec((tm, HEAD_DIM), lambda bb, i: (i, 0))] * 2
        args += [cos, sin_signed]
    osz = jnp.dtype(out_dtype).itemsize
    vm = 2 * tm * d * 4 + d * n * 2 + 2 * tm * n * osz + tm * d * 2 + 3 * tm * tn * 4
    return pl.pallas_call(
        functools.partial(_proj_body, tn=tn, rope_cols=rope_cols),
        out_shape=jax.ShapeDtypeStruct((b, t, n), out_dtype),
        grid=(b, t // tm),
        in_specs=in_specs,
        out_specs=pl.BlockSpec((None, tm, n), lambda bb, i: (bb, i, 0)),
        scratch_shapes=[pltpu.VMEM((tm, d), BF16)],
        compiler_params=_params(("arbitrary", "arbitrary"), vm),
        name="norm_mod_proj",
    )(*args)


def _oproj_body(a_ref, w_ref, res_ref, gate_ref, o_ref):
    y = jnp.dot(a_ref[...], w_ref[...], preferred_element_type=F32)
    o_ref[...] = res_ref[...] + gate_ref[...] * y


def _rec_oproj_body(yf_ref, yb_ref, gu_ref, w_ref, res_ref, gate_ref, o_ref):
    a = (yf_ref[...] + yb_ref[...]) * jax.nn.gelu(gu_ref[...])
    y = jnp.dot(a.astype(BF16), w_ref[...], preferred_element_type=F32)
    o_ref[...] = res_ref[...] + gate_ref[...] * y


def _oproj(a, w, res, gate):
    b, t, k = a.shape
    d = w.shape[1]
    tm = _tile(t, 512)
    vm = 2 * tm * k * 2 + 2 * k * d * 2 + 4 * tm * d * 4 + tm * d * 4
    return pl.pallas_call(
        _oproj_body,
        out_shape=jax.ShapeDtypeStruct((b, t, d), F32),
        grid=(b, t // tm),
        in_specs=[
            pl.BlockSpec((None, tm, k), lambda bb, i: (bb, i, 0)),
            pl.BlockSpec((k, d), lambda bb, i: (0, 0)),
            pl.BlockSpec((None, tm, d), lambda bb, i: (bb, i, 0)),
            pl.BlockSpec((None, 1, d), lambda bb, i: (bb, 0, 0)),
        ],
        out_specs=pl.BlockSpec((None, tm, d), lambda bb, i: (bb, i, 0)),
        compiler_params=_params(("arbitrary", "arbitrary"), vm),
        name="oproj_residual",
    )(a, w, res, gate)


def _rec_oproj(yf, yb, gu, w, res, gate):
    b, t, k = yf.shape
    d = w.shape[1]
    tm = _tile(t, 256)
    vm = 6 * tm * k * 4 + 2 * k * d * 2 + 4 * tm * d * 4 + 3 * tm * d * 4
    return pl.pallas_call(
        _rec_oproj_body,
        out_shape=jax.ShapeDtypeStruct((b, t, d), F32),
        grid=(b, t // tm),
        in_specs=[
            pl.BlockSpec((None, tm, k), lambda bb, i: (bb, i, 0)),
            pl.BlockSpec((None, tm, k), lambda bb, i: (bb, i, 0)),
            pl.BlockSpec((None, tm, k), lambda bb, i: (bb, i, 0)),
            pl.BlockSpec((k, d), lambda bb, i: (0, 0)),
            pl.BlockSpec((None, tm, d), lambda bb, i: (bb, i, 0)),
            pl.BlockSpec((None, 1, d), lambda bb, i: (bb, 0, 0)),
        ],
        out_specs=pl.BlockSpec((None, tm, d), lambda bb, i: (bb, i, 0)),
        compiler_params=_params(("arbitrary", "arbitrary"), vm),
        name="rec_oproj_residual",
    )(yf, yb, gu, w, res, gate)


def _stack_groups(q):
    return jnp.concatenate(
        [q[:, g * HEAD_DIM:(g + 1) * HEAD_DIM] for g in range(GQA_GROUP)], axis=0)


def _unstack_groups(o, rows):
    return jnp.concatenate([o[g * rows:(g + 1) * rows] for g in range(GQA_GROUP)], axis=1)


def _sink_column(sink_ref, h, rows):
    return jnp.concatenate(
        [jnp.full((rows, 1), sink_ref[h * GQA_GROUP + g], F32) for g in range(GQA_GROUP)], axis=0)


def _attn_body(sink_ref, q_ref, kp_ref, kc_ref, kn_ref, vp_ref, vc_ref, vn_ref, kx_ref, vx_ref,
               o_ref, *, nb):
    i = pl.program_id(1)
    blk = ATTN_BLOCK
    scale = HEAD_DIM ** -0.5
    r = lax.broadcasted_iota(jnp.int32, (GQA_GROUP * blk, blk), 0) % blk
    c = lax.broadcasted_iota(jnp.int32, (GQA_GROUP * blk, blk), 1)
    keep_p = (c >= r) & (i >= 1)
    keep_n = (c <= r) & (i + 1 < nb)
    neg = -jnp.inf
    qw = GQA_GROUP * HEAD_DIM
    for h in range(N_KV_HEADS):
        hs = slice(h * HEAD_DIM, (h + 1) * HEAD_DIM)
        qs = _stack_groups(q_ref[:, h * qw:(h + 1) * qw])
        s_p = jnp.where(keep_p, _dot_nt(qs, kp_ref[:, hs]) * scale, neg)
        s_c = _dot_nt(qs, kc_ref[:, hs]) * scale
        s_n = jnp.where(keep_n, _dot_nt(qs, kn_ref[:, hs]) * scale, neg)
        s_x = _dot_nt(qs, kx_ref[:, hs]) * scale
        sink = _sink_column(sink_ref, h, blk)
        m = jnp.maximum(
            jnp.maximum(jnp.maximum(s_p.max(-1, keepdims=True), s_c.max(-1, keepdims=True)),
                        jnp.maximum(s_n.max(-1, keepdims=True), s_x.max(-1, keepdims=True))),
            sink)
        p_p = jnp.exp(s_p - m)
        p_c = jnp.exp(s_c - m)
        p_n = jnp.exp(s_n - m)
        p_x = jnp.exp(s_x - m)
        denom = (p_p.sum(-1, keepdims=True) + p_c.sum(-1, keepdims=True)
                 + p_n.sum(-1, keepdims=True) + p_x.sum(-1, keepdims=True) + jnp.exp(sink - m))
        o = (jnp.dot(p_p.astype(BF16), vp_ref[:, hs], preferred_element_type=F32)
             + jnp.dot(p_c.astype(BF16), vc_ref[:, hs], preferred_element_type=F32)
             + jnp.dot(p_n.astype(BF16), vn_ref[:, hs], preferred_element_type=F32)
             + jnp.dot(p_x.astype(BF16), vx_ref[:, hs], preferred_element_type=F32))
        o_ref[:, h * qw:(h + 1) * qw] = _unstack_groups(o / denom, blk).astype(o_ref.dtype)


def _attention(qkv_x, qkv_c, sink):
    b, s, _ = qkv_x.shape
    c = qkv_c.shape[1]
    nb = s // ATTN_BLOCK
    q_cols = N_Q_HEADS * HEAD_DIM
    kv_cols = N_KV_HEADS * HEAD_DIM
    k0 = q_cols // kv_cols
    blk = ATTN_BLOCK

    def kv_spec(col, off):
        return pl.BlockSpec((None, blk, kv_cols),
                            lambda bb, i: (bb, jnp.clip(i + off, 0, nb - 1), col))

    in_specs = [
        pl.BlockSpec(memory_space=pltpu.SMEM),
        pl.BlockSpec((None, blk, q_cols), lambda bb, i: (bb, i, 0)),
        kv_spec(k0, -1), kv_spec(k0, 0), kv_spec(k0, 1),
        kv_spec(k0 + 1, -1), kv_spec(k0 + 1, 0), kv_spec(k0 + 1, 1),
        pl.BlockSpec((None, c, kv_cols), lambda bb, i: (bb, 0, k0)),
        pl.BlockSpec((None, c, kv_cols), lambda bb, i: (bb, 0, k0 + 1)),
    ]
    return pl.pallas_call(
        functools.partial(_attn_body, nb=nb),
        out_shape=jax.ShapeDtypeStruct((b, s, q_cols), BF16),
        grid=(b, nb),
        in_specs=in_specs,
        out_specs=pl.BlockSpec((None, blk, q_cols), lambda bb, i: (bb, i, 0)),
        compiler_params=_params(("arbitrary", "arbitrary"), 24 * 1024 * 1024),
        name="window_attention",
    )(sink, qkv_x, qkv_x, qkv_x, qkv_x, qkv_x, qkv_x, qkv_x, qkv_c, qkv_c)


def _ctx_attn_body(sink_ref, q_ref, k_ref, v_ref, o_ref):
    h = pl.program_id(1)
    rows = q_ref.shape[0]
    scale = HEAD_DIM ** -0.5
    qs = _stack_groups(q_ref[...])
    s = _dot_nt(qs, k_ref[...]) * scale
    sink = _sink_column(sink_ref, h, rows)
    m = jnp.maximum(s.max(-1, keepdims=True), sink)
    p = jnp.exp(s - m)
    denom = p.sum(-1, keepdims=True) + jnp.exp(sink - m)
    o = jnp.dot((p / denom).astype(BF16), v_ref[...], preferred_element_type=F32)
    o_ref[...] = _unstack_groups(o, rows).astype(o_ref.dtype)


def _ctx_attention(qkv_c, sink):
    b, c, _ = qkv_c.shape
    qw = GQA_GROUP * HEAD_DIM
    k0 = N_Q_HEADS
    v0 = N_Q_HEADS + N_KV_HEADS
    return pl.pallas_call(
        _ctx_attn_body,
        out_shape=jax.ShapeDtypeStruct((b, c, N_Q_HEADS * HEAD_DIM), BF16),
        grid=(b, N_KV_HEADS),
        in_specs=[
            pl.BlockSpec(memory_space=pltpu.SMEM),
            pl.BlockSpec((None, c, qw), lambda bb, h: (bb, 0, h)),
            pl.BlockSpec((None, c, HEAD_DIM), lambda bb, h: (bb, 0, k0 + h)),
            pl.BlockSpec((None, c, HEAD_DIM), lambda bb, h: (bb, 0, v0 + h)),
        ],
        out_specs=pl.BlockSpec((None, c, qw), lambda bb, h: (bb, 0, h)),
        compiler_params=_params(("arbitrary", "arbitrary"), 16 * 1024 * 1024),
        name="context_attention",
    )(sink, qkv_c, qkv_c, qkv_c)


def _block_diag(ub, w_ref, bias):
    bw = w_ref.shape[1]
    return jnp.concatenate(
        [jnp.dot(ub[:, n * bw:(n + 1) * bw], w_ref[n], preferred_element_type=F32)
         for n in range(w_ref.shape[0])], axis=1) + bias


def _lru_body(up_ref, uc_ref, un_ref, cw_ref, cb_ref, wa_ref, ba_ref, wx_ref, bx_ref, lam_ref,
              h0_ref, y_ref, hl_ref, h_scr, a_scr, b_scr, *, nt, reverse):
    i = pl.program_id(1)
    ti = (nt - 1 - i) if reverse else i
    tm = uc_ref.shape[0]
    halo = up_ref.shape[0]

    @pl.when(i == 0)
    def _():
        h_scr[...] = h0_ref[...]

    prev = jnp.where(ti > 0, up_ref[...], 0.0)
    nxt = jnp.where(ti < nt - 1, un_ref[...], 0.0)
    cur = uc_ref[...]

    def taps(window, rows):
        y = cb_ref[...]
        for k in range(CONV_W):
            off = halo - CONV_LEFT + k
            y = y + window[off:off + rows] * cw_ref[k:k + 1, :]
        return y

    u = cb_ref[...]
    for k in range(CONV_W):
        shift = (CONV_LEFT - k) % tm
        u = u + (cur if shift == 0 else pltpu.roll(cur, shift, 0)) * cw_ref[k:k + 1, :]
    u_lo = taps(jnp.concatenate([prev, cur[:2 * halo]], axis=0), halo)
    u_hi = taps(jnp.concatenate([cur[tm - 2 * halo:], nxt], axis=0), halo)
    u = jnp.concatenate([u_lo, u[halo:tm - halo], u_hi], axis=0)

    ub = u.astype(BF16)
    r = jax.nn.sigmoid(_block_diag(ub, wa_ref, ba_ref[...]))
    ig = jax.nn.sigmoid(_block_diag(ub, wx_ref, bx_ref[...]))
    nl = -lam_ref[...]
    softplus = jnp.maximum(nl, 0.0) + jnp.log1p(jnp.exp(-jnp.abs(nl)))
    log_a = -LRU_C * r * softplus
    a_scr[...] = jnp.exp(log_a)
    b_scr[...] = jnp.sqrt(1.0 - jnp.exp(2.0 * log_a)) * ig * u

    ng = tm // SUBLANES
    row = lax.broadcasted_iota(jnp.int32, (SUBLANES, a_scr.shape[1]), 0)

    def group(g, h):
        gi = (ng - 1 - g) if reverse else g
        r0 = pl.multiple_of(gi * SUBLANES, SUBLANES)
        a = a_scr[pl.ds(r0, SUBLANES), :]
        bb = b_scr[pl.ds(r0, SUBLANES), :]
        for k in (1, 2, 4):
            if reverse:
                keep = row < SUBLANES - k
                shift = SUBLANES - k
            else:
                keep = row >= k
                shift = k
            a_sh = pltpu.roll(a, shift, 0)
            b_sh = pltpu.roll(bb, shift, 0)
            bb = bb + a * jnp.where(keep, b_sh, 0.0)
            a = a * jnp.where(keep, a_sh, 1.0)
        y = bb + a * h
        y_ref[pl.ds(r0, SUBLANES), :] = y
        return y[0:1, :] if reverse else y[SUBLANES - 1:SUBLANES, :]

    h_last = lax.fori_loop(0, ng, group, h_scr[...])
    h_scr[...] = h_last

    @pl.when(i == nt - 1)
    def _():
        hl_ref[...] = h_last


def _lru_scan(gu, conv_w, conv_b, w_a, b_a, w_x, b_x, lam, h0, reverse):
    b, t, d2 = gu.shape
    d = d2 // 2
    tm = _tile(t, 256)
    nt = t // tm
    halo = SUBLANES
    hb = tm // halo
    nh = t // halo

    def tmap(i):
        return (nt - 1 - i) if reverse else i

    vec = lambda: pl.BlockSpec((1, d), lambda bb, i: (0, 0))
    in_specs = [
        pl.BlockSpec((None, halo, d), lambda bb, i: (bb, jnp.maximum(tmap(i) * hb - 1, 0), 1)),
        pl.BlockSpec((None, tm, d), lambda bb, i: (bb, tmap(i), 1)),
        pl.BlockSpec((None, halo, d), lambda bb, i: (bb, jnp.minimum((tmap(i) + 1) * hb, nh - 1), 1)),
        pl.BlockSpec((CONV_W, d), lambda bb, i: (0, 0)),
        vec(),
        pl.BlockSpec(w_a.shape, lambda bb, i: (0, 0, 0)),
        vec(),
        pl.BlockSpec(w_x.shape, lambda bb, i: (0, 0, 0)),
        vec(),
        vec(),
        pl.BlockSpec((None, 1, d), lambda bb, i: (bb, 0, 0)),
    ]
    vm = 2 * (tm + 2 * halo) * d * 4 + 2 * tm * d * 4 + 2 * tm * d * 4 + 8 * tm * d * 4 + 4 * w_a.size * 2
    return pl.pallas_call(
        functools.partial(_lru_body, nt=nt, reverse=reverse),
        out_shape=(jax.ShapeDtypeStruct((b, t, d), F32), jax.ShapeDtypeStruct((b, 1, d), F32)),
        grid=(b, nt),
        in_specs=in_specs,
        out_specs=(pl.BlockSpec((None, tm, d), lambda bb, i: (bb, tmap(i), 0)),
                   pl.BlockSpec((None, 1, d), lambda bb, i: (bb, 0, 0))),
        scratch_shapes=[pltpu.VMEM((1, d), F32), pltpu.VMEM((tm, d), F32), pltpu.VMEM((tm, d), F32)],
        compiler_params=_params(("arbitrary", "arbitrary"), vm),
        name="rglru_bwd" if reverse else "rglru_fwd",
    )(gu, gu, gu, conv_w, conv_b.reshape(1, d), w_a, b_a.reshape(1, d), w_x, b_x.reshape(1, d),
      lam.reshape(1, d), h0)


def _peer_query_body(x_ref, g_ref, sh_ref, sc_ref, wq_ref, keys_ref, ht_ref, st_ref):
    h = _norm_mod(x_ref[...], g_ref[...], sh_ref[...], sc_ref[...])
    hb = h.astype(BF16)
    ht = h.T.astype(BF16)
    for p in range(ht_ref.shape[0]):
        ht_ref[p] = ht[:, p * PEER_PIECE:(p + 1) * PEER_PIECE]
    q = jnp.dot(hb, wq_ref[...], preferred_element_type=F32).astype(BF16)
    for hp in range(2 * PEER_HEADS):
        qc = q[:, hp * PEER_KEY_DIM:(hp + 1) * PEER_KEY_DIM]
        st_ref[hp] = _dot_nt(keys_ref[hp % 2], qc)


def _peer_query(x, gain, shift, scale, wq, keys):
    b, t, d = x.shape
    tm = _tile(t, PEER_SUB)
    nt = t // tm
    nq = wq.shape[1]
    vm = 2 * tm * d * 4 + 2 * d * nq * 2 + 2 * d * tm * 2 + 2 * 2 * PEER_HEADS * N_KEYS * tm * 4 + 4 * tm * d * 4
    return pl.pallas_call(
        _peer_query_body,
        out_shape=(jax.ShapeDtypeStruct((b * nt, tm // PEER_PIECE, d, PEER_PIECE), BF16),
                   jax.ShapeDtypeStruct((2 * PEER_HEADS, N_KEYS, b * t), F32)),
        grid=(b, nt),
        in_specs=[
            pl.BlockSpec((None, tm, d), lambda bb, i: (bb, i, 0)),
            pl.BlockSpec((1, d), lambda bb, i: (0, 0)),
            pl.BlockSpec((None, 1, d), lambda bb, i: (bb, 0, 0)),
            pl.BlockSpec((None, 1, d), lambda bb, i: (bb, 0, 0)),
            pl.BlockSpec((d, nq), lambda bb, i: (0, 0)),
            pl.BlockSpec(keys.shape, lambda bb, i: (0, 0, 0)),
        ],
        out_specs=(pl.BlockSpec((None, tm // PEER_PIECE, d, PEER_PIECE),
                                lambda bb, i: (bb * nt + i, 0, 0, 0)),
                   pl.BlockSpec((2 * PEER_HEADS, N_KEYS, tm), lambda bb, i: (0, 0, bb * nt + i))),
        compiler_params=_params(("arbitrary", "arbitrary"), vm),
        name="peer_query",
    )(x, gain.reshape(1, d), shift, scale, wq, keys)


def _oddeven_merge(lo, hi, r):
    step = r * 2
    if step < hi - lo:
        yield from _oddeven_merge(lo, hi, step)
        yield from _oddeven_merge(lo + r, hi, step)
        yield from [(k, k + r) for k in range(lo + r, hi - r, step)]
    else:
        yield (lo, lo + r)


def _oddeven_sort(lo, hi):
    if hi - lo >= 1:
        mid = lo + (hi - lo) // 2
        yield from _oddeven_sort(lo, mid)
        yield from _oddeven_sort(mid + 1, hi)
        yield from _oddeven_merge(lo, hi, 1)


_SORT16 = tuple(_oddeven_sort(0, PEER_TOPK - 1))


def _exchange(x, p, q):
    hi = jnp.maximum(x[p], x[q])
    lo = jnp.minimum(x[p], x[q])
    x[p] = hi
    x[q] = lo


def _merge_sublanes(x):
    n = len(x)
    shift = SUBLANES // 2
    while shift >= 1:
        z = [jnp.maximum(x[k], pltpu.roll(x[n - 1 - k], shift, 0)) for k in range(n)]
        dist = n // 2
        while dist >= 1:
            for k in range(n):
                if k & dist == 0:
                    _exchange(z, k, k + dist)
            dist //= 2
        x = z
        shift //= 2
    return x


def _top16_sorted(s):
    x = [s[SUBLANES * v:SUBLANES * (v + 1)] for v in range(s.shape[0] // SUBLANES)]
    assert len(x) == PEER_TOPK
    for p, q in _SORT16:
        _exchange(x, p, q)
    return _merge_sublanes(x)


def _dup16(v):
    bits = pltpu.bitcast(v.astype(BF16).astype(F32), jnp.uint32) >> 16
    return bits | (bits << 16)


def _route_stats(st_ref, e1d_scr, c1d_scr, e2_scr, r2_scr):
    tm = st_ref.shape[2]
    row = lax.broadcasted_iota(jnp.int32, (SUBLANES, tm), 0)
    for h in range(PEER_HEADS):
        s1 = st_ref[2 * h]
        s2 = st_ref[2 * h + 1]
        a = _top16_sorted(s1)
        b = _top16_sorted(s2)
        a_lo = a[SUBLANES - 1]
        a_hi = a[2 * SUBLANES - 1]
        for i in range(SUBLANES - 2, -1, -1):
            a_lo = jnp.where(row == i, a[i], a_lo)
            a_hi = jnp.where(row == i, a[SUBLANES + i], a_hi)
        c = [a_lo + b[j] for j in range(PEER_TOPK)]
        d = a_hi + b[0]
        c = [jnp.maximum(c[0], d)] + [
            jnp.maximum(c[j], jnp.minimum(c[j - 1], d)) for j in range(1, PEER_TOPK)]
        t = _merge_sublanes(c)
        z = jnp.ones_like(t[0])
        for k in range(1, PEER_TOPK):
            z = z + jnp.exp(t[k] - t[0])
        inv_z = 1.0 / z
        tau = t[PEER_TOPK - 1][0:1]
        count1 = jnp.zeros_like(s1)
        rank2 = jnp.zeros_like(s2)
        for j in range(PEER_TOPK):
            bj = b[j][0:1]
            count1 = count1 + jnp.where(s1 + bj >= tau, 1.0, 0.0)
            rank2 = rank2 + jnp.where(bj > s2, 1.0, 0.0)
        e1 = jnp.exp(s1 - a[0][0:1]) * inv_z[0:1]
        e2 = jnp.exp(s2 - b[0][0:1])
        for cc in range(tm // LANES):
            cs = slice(cc * LANES, (cc + 1) * LANES)
            c1d_scr[h, cc] = _dup16(count1[:, cs])
            e1d_scr[h, cc] = _dup16(e1[:, cs])
            r2_scr[h, cc] = rank2[:, cs].astype(BF16)
            e2_scr[h, cc] = e2[:, cs].astype(BF16)


def _route_body(st_ref, e1d_ref, c1d_ref, e2_ref, r2_ref):
    _route_stats(st_ref, e1d_ref, c1d_ref, e2_ref, r2_ref)


def _peer_route(st):
    t = st.shape[2]
    tm = _tile(t, PEER_SUB)
    nc = tm // LANES
    shape = (PEER_HEADS, t // LANES, N_KEYS, LANES)
    spec = pl.BlockSpec((PEER_HEADS, nc, N_KEYS, LANES), lambda i: (0, i, 0, 0))
    stat = PEER_HEADS * N_KEYS * tm
    return pl.pallas_call(
        _route_body,
        out_shape=(jax.ShapeDtypeStruct(shape, jnp.uint32), jax.ShapeDtypeStruct(shape, jnp.uint32),
                   jax.ShapeDtypeStruct(shape, BF16), jax.ShapeDtypeStruct(shape, BF16)),
        grid=(t // tm,),
        in_specs=[pl.BlockSpec((2 * PEER_HEADS, N_KEYS, tm), lambda i: (0, 0, i))],
        out_specs=(spec, spec, spec, spec),
        compiler_params=_params(("arbitrary",), 2 * 2 * stat * 4 + 2 * stat * 12 + 8 * stat),
        name="peer_route",
    )(st)


def _dense_act(zt_ref, act_ref, blk, c0, cols, e1d_ref, c1d_ref, e2_ref, r2_ref):
    eb = zt_ref.shape[1]
    for l in range(eb // N_KEYS):
        i1 = jnp.clip(blk * (eb // N_KEYS) + l, 0, N_KEYS - 1)
        rs = slice(l * N_KEYS, (l + 1) * N_KEYS)
        for c in cols:
            g = None
            for h in range(PEER_HEADS):
                cnt = jnp.broadcast_to(c1d_ref[h, c0 + c, pl.ds(i1, 1), :], (N_KEYS // 2, LANES))
                e1 = jnp.broadcast_to(e1d_ref[h, c0 + c, pl.ds(i1, 1), :], (N_KEYS // 2, LANES))
                gh = jnp.where(r2_ref[h, c0 + c] < pltpu.bitcast(cnt, BF16),
                               e2_ref[h, c0 + c] * pltpu.bitcast(e1, BF16), jnp.zeros((), BF16))
                g = gh if g is None else g + gh
            act_ref[c, rs, :] = jax.nn.gelu(zt_ref[c, rs, :].astype(BF16)) * g


def _peer_expert_body(ht_ref, e1d_hbm, c1d_hbm, e2_hbm, r2_hbm, u_ref, vt_ref, o_ref,
                      e1d_scr, c1d_scr, e2_scr, r2_scr, gate_sem, zt_scr, act_scr, *, eb):
    j = pl.program_id(1)
    ns = pl.num_programs(1) - 1
    gates = (e1d_scr, c1d_scr, e2_scr, r2_scr)
    n_sub, n_pc, _, pw = ht_ref.shape
    cpp = pw // LANES
    nc = n_pc * cpp

    def cols_of(p):
        return range(p * cpp, (p + 1) * cpp)

    def stage_a(k, s, p):
        z = jnp.dot(u_ref[s * eb:(s + 1) * eb, :], ht_ref[k, p], preferred_element_type=F32)
        for n, c in enumerate(cols_of(p)):
            zt_scr[k, s, c] = z[:, n * LANES:(n + 1) * LANES]

    def stage_b(k, s, p):
        _dense_act(zt_scr.at[k, 1 - s], act_scr.at[k, 1 - s], 2 * j - 1 + s, k * nc, cols_of(p),
                   *gates)

    def stage_c(k, s, p):
        act = jnp.concatenate([act_scr[k, s, c] for c in cols_of(p)], axis=1)
        o_ref[k, p] += jnp.dot(vt_ref[s], act, preferred_element_type=F32)

    def first_step(k, carry):
        for p in range(n_pc):
            stage_a(k, 0, p)
        for p in range(n_pc):
            stage_b(k, 1, p)
            stage_a(k, 1, p)
        return carry

    def middle_step(k, carry):
        for s in (0, 1):
            for p in range(n_pc):
                stage_c(k, s, p)
                stage_b(k, s, p)
                stage_a(k, s, p)
        return carry

    def last_step(k, carry):
        for p in range(n_pc):
            stage_c(k, 0, p)
            stage_b(k, 0, p)
        for p in range(n_pc):
            stage_c(k, 1, p)
        return carry

    @pl.when(j == 0)
    def _():
        gnc = e1d_scr.shape[1]
        first = pl.multiple_of(pl.program_id(0) * gnc, gnc)
        copies = [
            pltpu.make_async_copy(src.at[:, pl.ds(first, gnc)], dst, gate_sem.at[n])
            for n, (src, dst) in enumerate(zip((e1d_hbm, c1d_hbm, e2_hbm, r2_hbm), gates))]
        for cp in copies:
            cp.start()
        o_ref[...] = jnp.zeros_like(o_ref)
        for cp in copies:
            cp.wait()
        lax.fori_loop(0, n_sub, first_step, 0)

    @pl.when((j > 0) & (j < ns))
    def _():
        lax.fori_loop(0, n_sub, middle_step, 0)

    @pl.when(j == ns)
    def _():
        lax.fori_loop(0, n_sub, last_step, 0)


def _peer_experts(ht, gates, u, vt):
    n_tiles, n_pc, d, pw = ht.shape
    tm = n_pc * pw
    e = u.shape[0]
    g = 2 if n_tiles % 2 == 0 else 1
    nc = tm // LANES
    eb = PEER_EB
    ns = e // (2 * eb)
    assert vt.shape == (ns, 2, d, eb)
    once = pl.Buffered(1)
    gate_spec = pl.BlockSpec(memory_space=pl.ANY)
    gate_shape = (PEER_HEADS, g * nc, N_KEYS, LANES)
    in_specs = [
        pl.BlockSpec((g, n_pc, d, pw), lambda i, j: (i, 0, 0, 0), pipeline_mode=once),
        gate_spec, gate_spec, gate_spec, gate_spec,
        pl.BlockSpec((2 * eb, d), lambda i, j: (jnp.minimum(j, ns - 1), 0)),
        pl.BlockSpec((None, 2, d, eb), lambda i, j: (jnp.maximum(j - 1, 0), 0, 0, 0)),
    ]
    stat = PEER_HEADS * N_KEYS * tm
    vm = g * (d * tm * 2 + stat * 12 + 2 * d * tm * 4 + 2 * eb * tm * 6) + 8 * eb * d * 2
    return pl.pallas_call(
        functools.partial(_peer_expert_body, eb=eb),
        out_shape=jax.ShapeDtypeStruct((n_tiles, n_pc, d, pw), F32),
        grid=(n_tiles // g, ns + 1),
        in_specs=in_specs,
        out_specs=pl.BlockSpec((g, n_pc, d, pw), lambda i, j: (i, 0, 0, 0)),
        scratch_shapes=[
            pltpu.VMEM(gate_shape, jnp.uint32),
            pltpu.VMEM(gate_shape, jnp.uint32),
            pltpu.VMEM(gate_shape, BF16),
            pltpu.VMEM(gate_shape, BF16),
            pltpu.SemaphoreType.DMA((4,)),
            pltpu.VMEM((g, 2, nc, eb, LANES), F32),
            pltpu.VMEM((g, 2, nc, eb, LANES), BF16),
        ],
        compiler_params=_params(("arbitrary", "arbitrary"), vm),
        name="peer_experts",
    )(ht, *gates, u, vt)


def _peer_residual_body(*refs, final_norm):
    if final_norm:
        xs_ref, gate_ref, ft_ref, gain_ref, o_ref = refs
    else:
        xs_ref, gate_ref, ft_ref, o_ref = refs
    f = jnp.concatenate([ft_ref[p].T for p in range(ft_ref.shape[0])], axis=0)
    y = xs_ref[...] + gate_ref[...] * f
    if final_norm:
        ms = jnp.mean(y * y, axis=-1, keepdims=True)
        y = (y * lax.rsqrt(ms + NORM_EPS)) * gain_ref[...]
    o_ref[...] = y


def _peer_residual(xs, gate, ft, final_gain=None):
    b, t, d = xs.shape
    n_pc, pw = ft.shape[1], ft.shape[3]
    tm = n_pc * pw
    nt = t // tm
    final_norm = final_gain is not None
    in_specs = [
        pl.BlockSpec((None, tm, d), lambda bb, i: (bb, i, 0)),
        pl.BlockSpec((None, 1, d), lambda bb, i: (bb, 0, 0)),
        pl.BlockSpec((None, n_pc, d, pw), lambda bb, i: (bb * nt + i, 0, 0, 0)),
    ]
    args = [xs, gate, ft]
    if final_norm:
        in_specs.append(pl.BlockSpec((1, d), lambda bb, i: (0, 0)))
        args.append(final_gain.reshape(1, d))
    return pl.pallas_call(
        functools.partial(_peer_residual_body, final_norm=final_norm),
        out_shape=jax.ShapeDtypeStruct((b, t, d), F32),
        grid=(b, nt),
        in_specs=in_specs,
        out_specs=pl.BlockSpec((None, tm, d), lambda bb, i: (bb, i, 0)),
        compiler_params=_params(("arbitrary", "arbitrary"), 8 * tm * d * 4),
        name="peer_residual",
    )(*args)


def _tables_body(u_ref, v_ref, ub_ref, vt_ref):
    ub_ref[...] = u_ref[...].astype(BF16)
    vt_ref[...] = v_ref[...].T.astype(BF16)


def _peer_tables(u, v, layer):
    _, e, d = u.shape
    eb = PEER_EB
    table = pl.BlockSpec((None, eb, d), lambda i: (layer, i, 0))
    return pl.pallas_call(
        _tables_body,
        out_shape=(jax.ShapeDtypeStruct((e, d), BF16),
                   jax.ShapeDtypeStruct((e // (2 * eb), 2, d, eb), BF16)),
        grid=(e // eb,),
        in_specs=[table, table],
        out_specs=(pl.BlockSpec((eb, d), lambda i: (i, 0)),
                   pl.BlockSpec((None, None, d, eb), lambda i: (i // 2, i % 2, 0, 0))),
        compiler_params=_params(("arbitrary",), 2 * 2 * eb * d * 4 + 2 * 2 * eb * d * 2 + 2 * eb * d * 4),
        name="peer_tables",
    )(u, v)


def _peer(xs, gain, shift, scale, gate, wq, keys, u, vt, final_gain=None):
    ht, st = _peer_query(xs, gain, shift, scale, wq, keys)
    ft = _peer_experts(ht, _peer_route(st), u, vt)
    return _peer_residual(xs, gate, ft, final_gain)


def _rope_tables(s):
    t = jnp.arange(s)
    row = (t // GRID_W).astype(F32)
    col = (t % GRID_W).astype(F32)
    half = HEAD_DIM // 2
    inv = ROPE_BASE ** (-jnp.arange(0, half, 2, dtype=F32) / half)
    ang_r = row[:, None] * inv[None, :]
    ang_c = col[:, None] * inv[None, :]
    ang = jnp.concatenate([ang_r, ang_r, ang_c, ang_c], axis=-1)
    lane = jnp.arange(HEAD_DIM)
    sign = jnp.where((lane % half) < half // 2, -1.0, 1.0).astype(F32)
    return jnp.cos(ang), jnp.sin(ang) * sign[None, :]


def kernel(x, c, ctx, c_ctx, w_mod, b_mod, norm_mix, norm_ffn, norm_final, attn_w_qkv, attn_w_o, attn_sink, rec_w_in, rec_conv_w, rec_conv_b, rec_w_a, rec_b_a, rec_w_x, rec_b_x, rec_lambda, rec_w_out, peer_w_q, peer_keys, peer_u, peer_v):
    b, s, d = x.shape
    cl = ctx.shape[1]
    depth = w_mod.shape[0]
    rows = -(-(b + 1) // SUBLANES) * SUBLANES
    cvec = jnp.zeros((rows, d), F32).at[:b].set(c).at[b].set(c_ctx)
    mod = _mod_vectors(cvec, w_mod, b_mod).reshape(depth, rows, N_MOD, d)
    cos, sin_signed = _rope_tables(s)
    q_cols = N_Q_HEADS * HEAD_DIM
    kv_cols = N_KV_HEADS * HEAD_DIM

    xs = x
    cs = ctx.reshape(1, b * cl, d)
    for i in range(depth):
        last = i == depth - 1
        j = i // 2
        mx = [mod[i, :b, k][:, None, :] for k in range(N_MOD)]
        mc = [mod[i, b:b + 1, k][:, None, :] for k in range(N_MOD)]

        if i % 2 == 0:
            wqkv = attn_w_qkv[j].astype(BF16)
            wo = attn_w_o[j].astype(BF16)
            qkv_x = _proj(xs, norm_mix[i], mx[0], mx[1], wqkv, BF16, 512,
                          rope=(cos, sin_signed, q_cols + kv_cols))
            qkv_c = _proj(cs, norm_mix[i], mc[0], mc[1], wqkv, BF16, 512).reshape(b, cl, -1)
            o_x = _attention(qkv_x, qkv_c, attn_sink[j])
            xs = _oproj(o_x, wo, xs, mx[2])
            if not last:
                o_c = _ctx_attention(qkv_c, attn_sink[j]).reshape(1, b * cl, q_cols)
                cs = _oproj(o_c, wo, cs, mc[2])
        else:
            w_in = rec_w_in[j].astype(BF16)
            w_out = rec_w_out[j].astype(BF16)
            w_a = rec_w_a[j].astype(BF16)
            w_x = rec_w_x[j].astype(BF16)
            gu_x = _proj(xs, norm_mix[i], mx[0], mx[1], w_in, F32, 1024)
            gu_c = _proj(cs, norm_mix[i], mc[0], mc[1], w_in, F32, 1024).reshape(b, cl, -1)
            h0 = jnp.zeros((b, 1, d), F32)
            ys_c, ys_x = [], []
            for r in range(2):
                lru = functools.partial(
                    _lru_scan, conv_w=rec_conv_w[j], conv_b=rec_conv_b[j], w_a=w_a[r],
                    b_a=rec_b_a[j, r], w_x=w_x[r], b_x=rec_b_x[j, r], lam=rec_lambda[j, r],
                    reverse=(r == 1))
                y_c, h_c = lru(gu_c, h0=h0)
                y_x, _ = lru(gu_x, h0=h_c)
                ys_c.append(y_c)
                ys_x.append(y_x)
            xs = _rec_oproj(ys_x[0], ys_x[1], gu_x, w_out, xs, mx[2])
            if not last:
                cs = _rec_oproj(ys_c[0].reshape(1, b * cl, d), ys_c[1].reshape(1, b * cl, d),
                                gu_c.reshape(1, b * cl, -1), w_out, cs, mc[2])

        wq = peer_w_q[i].astype(BF16)
        keys = peer_keys[i].astype(BF16)
        u, vt = _peer_tables(peer_u, peer_v, i)
        xs = _peer(xs, norm_ffn[i], mx[3], mx[4], mx[5], wq, keys, u, vt,
                   final_gain=norm_final if last else None)
        if not last:
            cs = _peer(cs, norm_ffn[i], mc[3], mc[4], mc[5], wq, keys, u, vt)
    return xs
```
